```python
import functools
import math
import jax
import jax.numpy as jnp
from jax import lax
import numpy as np

D_MODEL = 1024
BATCH = 2
SEQ = 8192
DEPTH = 4
DEC_BATCH = 128
DEC_SEQ = 8
PAST_LEN = 8192
PAGE_SIZE = 128

N_EVEN = (DEPTH + 1) // 2
N_ODD = DEPTH // 2
QBLK = 128
ROPE_THETA = 10000.0
EPS = 1e-6
NEG = -1e30
FORCE_SCORE = 1e9
FAR_POS = -(2 ** 30)

A_HEADS = 8
A_DH = 64
A_KVH = 2
IDX_HEADS = 8
IDX_DH = 64
IDX_TOPK_MAX = 256
B_HEADS = 8
B_NOPE = 64
B_ROPE = 32
B_VDH = 64
B_QLORA = 256
B_KVLORA = 128
C_HEADS = 8
C_DH = 64
CMP_BLK = 32
SEL_BLK = 64
SEL_RATIO = SEL_BLK // CMP_BLK
N_SEL = 16
WINDOW = 512
D_HEADS = 4
D_DH = 64
D_KVH = 2
N_GROUPS = 4
E_PER_GROUP = 4
N_EXPERTS = N_GROUPS * E_PER_GROUP
TOP_E = 2
E_HIDDEN = 256

EVEN_SPLITS = (A_HEADS * A_DH, A_KVH * A_DH, A_KVH * A_DH, IDX_HEADS * IDX_DH, IDX_HEADS, IDX_DH, B_QLORA, B_KVLORA + B_ROPE)
EVEN_COLS = sum(EVEN_SPLITS)
ODD_SPLITS = (C_HEADS * C_DH, 2 * C_DH, 2 * C_DH, 2 * C_DH, 3 * C_HEADS, D_HEADS * 2 * D_DH, D_KVH * 2 * D_DH, D_KVH * 2 * D_DH)
ODD_COLS = sum(ODD_SPLITS)

kernel_name = 'hybrid_dsa_mla_nsa_diff_hmoe_step'


def split_cols(y, sizes):
    return jnp.split(y, [int(v) for v in np.cumsum(sizes)[:-1]], axis=-1)


def rms_norm(x, g):
    xf = x.astype(jnp.float32)
    y = xf * lax.rsqrt(jnp.mean(xf * xf, axis=-1, keepdims=True) + EPS)
    return (y * g.astype(jnp.float32)).astype(x.dtype)


def rope(x, pos):
    half = x.shape[-1] // 2
    inv = ROPE_THETA ** (-jnp.arange(half, dtype=jnp.float32) / half)
    ang = pos.astype(jnp.float32)[:, None] * inv[None, :]
    shape = (1, pos.shape[0]) + (1,) * (x.ndim - 3) + (half,)
    cos = jnp.cos(ang).reshape(shape)
    sin = jnp.sin(ang).reshape(shape)
    xf = x.astype(jnp.float32)
    x1, x2 = xf[..., :half], xf[..., half:]
    return jnp.concatenate([x1 * cos - x2 * sin, x2 * cos + x1 * sin], axis=-1).astype(x.dtype)


def masked_softmax(s, mask):
    s = jnp.where(mask, s.astype(jnp.float32), NEG)
    e = jnp.exp(s - jnp.max(s, axis=-1, keepdims=True)) * mask
    return e / jnp.maximum(jnp.sum(e, axis=-1, keepdims=True), 1e-30)


def sweep_queries(fn, q_inputs, q_pos):
    tq = q_pos.shape[0]
    if tq <= QBLK or tq % QBLK:
        return fn(*q_inputs, q_pos)
    nb = tq // QBLK
    to_blocks = lambda a: jnp.moveaxis(a.reshape((a.shape[0], nb, QBLK) + a.shape[2:]), 1, 0)
    xs = tuple(to_blocks(a) for a in q_inputs) + (q_pos.reshape(nb, QBLK),)
    out = jnp.moveaxis(lax.map(lambda a: fn(*a), xs), 0, 1)
    return out.reshape((out.shape[0], tq) + out.shape[3:])


def paged(pool, li, page_table):
    g = pool[li, page_table]
    return g.reshape((page_table.shape[0], page_table.shape[1] * pool.shape[2]) + pool.shape[3:])


def dsa_attend(q, qi, wi, k, v, ki, k_top, q_pos):
    B, Tq = q.shape[:2]
    S = k.shape[1]
    causal = jnp.arange(S, dtype=jnp.int32)[None, :] <= q_pos[:, None]
    idx_s = jnp.einsum('bthd,bsd->bths', qi, ki, preferred_element_type=jnp.float32)
    score = jnp.einsum('bths,bth->bts', jax.nn.relu(idx_s), wi.astype(jnp.float32))
    score = jnp.where(causal[None], score, -jnp.inf)
    _, sel = lax.top_k(score, k_top)
    bidx = jnp.arange(B)[:, None, None]
    ks = k[bidx, sel]
    vs = v[bidx, sel]
    qg = q.reshape(B, Tq, A_KVH, A_HEADS // A_KVH, A_DH)
    s = jnp.einsum('btgrd,btkgd->btgrk', qg, ks, preferred_element_type=jnp.float32) * A_DH ** -0.5
    p = masked_softmax(s, (sel <= q_pos[None, :, None])[:, :, None, None, :])
    o = jnp.einsum('btgrk,btkgd->btgrd', p.astype(vs.dtype), vs)
    return o.reshape(B, Tq, A_HEADS * A_DH)


def mla_attend(q_nope, q_rope, ckv, kr, w_uk, w_uv, q_pos):
    B, Tq = q_nope.shape[:2]
    S = ckv.shape[1]
    q_lat = jnp.einsum('bthn,chn->bthc', q_nope, w_uk)
    s = (jnp.einsum('bthc,bsc->bhts', q_lat, ckv, preferred_element_type=jnp.float32)
         + jnp.einsum('bthr,bsr->bhts', q_rope, kr, preferred_element_type=jnp.float32)) * (B_NOPE + B_ROPE) ** -0.5
    p = masked_softmax(s, jnp.arange(S, dtype=jnp.int32)[None, :] <= q_pos[:, None])
    o_lat = jnp.einsum('bhts,bsc->bthc', p.astype(ckv.dtype), ckv)
    o = jnp.einsum('bthc,chv->bthv', o_lat, w_uv)
    return o.reshape(B, Tq, B_HEADS * B_VDH)


def compress_blocks(x, phi, pe):
    B, S, dh = x.shape
    n = S // CMP_BLK
    blocks = x[:, :n * CMP_BLK].reshape(B, n, CMP_BLK, dh) + pe
    return jnp.einsum('bnld,lde->bne', blocks, phi)


def nsa_attend(q, q_r, gates, kcmp, vcmp, cmp_end, ks_blk, vs_blk, kw_pad, vw_pad, kw_pos, win_pos0, q_pos):
    B, Tq = q.shape[:2]
    scale = C_DH ** -0.5
    s_c = jnp.einsum('bthd,bnd->bthn', q, kcmp, preferred_element_type=jnp.float32) * scale
    p_c = masked_softmax(s_c, (cmp_end[None, :] <= q_pos[:, None])[None, :, None, :])
    o_c = jnp.einsum('bthn,bnd->bthd', p_c.astype(vcmp.dtype), vcmp)
    n_sel = ks_blk.shape[1]
    imp = jnp.sum(p_c, axis=2)
    imp = jnp.pad(imp, ((0, 0), (0, 0), (0, n_sel * SEL_RATIO - imp.shape[-1])))
    imp = imp.reshape(B, Tq, n_sel, SEL_RATIO).sum(-1)
    blk = jnp.arange(n_sel, dtype=jnp.int32)
    forced = (blk[None, :] == (q_pos // SEL_BLK)[:, None]) | (blk[None, :] == 0)
    visible = blk[None, :] * SEL_BLK <= q_pos[:, None]
    imp = jnp.where(visible[None], jnp.where(forced[None], FORCE_SCORE, imp), -jnp.inf)
    n_top = min(N_SEL, n_sel)
    _, sel = lax.top_k(imp, n_top)
    bidx = jnp.arange(B)[:, None, None]
    ksg = ks_blk[bidx, sel].reshape(B, Tq, n_top * SEL_BLK, C_DH)
    vsg = vs_blk[bidx, sel].reshape(B, Tq, n_top * SEL_BLK, C_DH)
    spos = (sel[..., None] * SEL_BLK + jnp.arange(SEL_BLK, dtype=jnp.int32)).reshape(B, Tq, n_top * SEL_BLK)
    s_s = jnp.einsum('bthd,btkd->bthk', q_r, ksg, preferred_element_type=jnp.float32) * scale
    p_s = masked_softmax(s_s, (spos <= q_pos[None, :, None])[:, :, None, :])
    o_s = jnp.einsum('bthk,btkd->bthd', p_s.astype(vsg.dtype), vsg)
    wlen = WINDOW + Tq
    start = q_pos[0] - win_pos0
    kw = lax.dynamic_slice_in_dim(kw_pad, start, wlen, axis=1)
    vw = lax.dynamic_slice_in_dim(vw_pad, start, wlen, axis=1)
    wpos = lax.dynamic_slice_in_dim(kw_pos, start, wlen, axis=0)
    dist = q_pos[:, None] - wpos[None, :]
    s_w = jnp.einsum('bthd,bsd->bths', q_r, kw, preferred_element_type=jnp.float32) * scale
    p_w = masked_softmax(s_w, ((dist >= 0) & (dist < WINDOW))[None, :, None, :])
    o_w = jnp.einsum('bths,bsd->bthd', p_w.astype(vw.dtype), vw)
    o = gates[..., 0:1] * o_c + gates[..., 1:2] * o_s + gates[..., 2:3] * o_w
    return o.reshape(B, Tq, C_HEADS * C_DH)


def diff_attend(q, k_full, v_full, lam, lam_init, subln, q_pos):
    B, Tq = q.shape[:2]
    S = k_full.shape[1]
    qg = q.reshape(B, Tq, D_KVH, D_HEADS // D_KVH, 2, D_DH)
    k = k_full.reshape(B, S, D_KVH, 2, D_DH)
    s = jnp.einsum('btgrmd,bsgmd->bgrmts', qg, k, preferred_element_type=jnp.float32) * D_DH ** -0.5
    p = masked_softmax(s, jnp.arange(S, dtype=jnp.int32)[None, :] <= q_pos[:, None])
    a = p[:, :, :, 0] - lam * p[:, :, :, 1]
    o = jnp.einsum('bgrts,bsgd->btgrd', a.astype(v_full.dtype), v_full)
    o = rms_norm(o, subln) * (1.0 - lam_init)
    return o.reshape(B, Tq, D_HEADS * 2 * D_DH)


def even_mixer(h, past, w_in, q_norm, kv_norm, w_qb, w_kvb):
    B, T, _ = h.shape
    past_len = 0 if past is None else past[0].shape[1]
    pos = past_len + jnp.arange(T, dtype=jnp.int32)
    qa, ka, va, qi, wi, ki, qlat, kvlat = split_cols(jnp.einsum('btd,de->bte', h, w_in), EVEN_SPLITS)
    qa = rope(qa.reshape(B, T, A_HEADS, A_DH), pos)
    ka = rope(ka.reshape(B, T, A_KVH, A_DH), pos)
    va = va.reshape(B, T, A_KVH, A_DH)
    qi = rope(qi.reshape(B, T, IDX_HEADS, IDX_DH), pos)
    ki = rope(ki, pos)
    wi = wi * (IDX_HEADS ** -0.5 * IDX_DH ** -0.5)
    qb = jnp.einsum('btc,chn->bthn', rms_norm(qlat, q_norm), w_qb)
    q_nope = qb[..., :B_NOPE]
    q_rope = rope(qb[..., B_NOPE:], pos)
    c_kv = rms_norm(kvlat[..., :B_KVLORA], kv_norm)
    k_rope = rope(kvlat[..., B_KVLORA:], pos)
    new_rows = (jnp.stack([ka, va], axis=2), ki, c_kv, k_rope)
    if past is None:
        kv_full, ki_full, ckv_full, kr_full = new_rows
    else:
        kv_full, ki_full, ckv_full, kr_full = (jnp.concatenate([p_, n_], axis=1) for p_, n_ in zip(past, new_rows))
    k_full, v_full = kv_full[:, :, 0], kv_full[:, :, 1]
    k_top = min(IDX_TOPK_MAX, ki_full.shape[1] // 4)
    w_uk, w_uv = w_kvb[..., :B_NOPE], w_kvb[..., B_NOPE:]

    def block(qa_b, qi_b, wi_b, qn_b, qr_b, q_pos):
        oa = dsa_attend(qa_b, qi_b, wi_b, k_full, v_full, ki_full, k_top, q_pos)
        ob = mla_attend(qn_b, qr_b, ckv_full, kr_full, w_uk, w_uv, q_pos)
        return jnp.concatenate([oa, ob], axis=-1)

    out = sweep_queries(block, (qa, qi, wi, q_nope, q_rope), pos)
    return out, new_rows


def odd_mixer(h, past, w_in, phi_k, phi_v, pe_k, pe_v, lam_vec, subln, lam_init):
    B, T, _ = h.shape
    past_len = 0 if past is None else past[0].shape[1]
    pos = past_len + jnp.arange(T, dtype=jnp.int32)
    qc, kvc, kvs, kvw, gc, qd, kd, vd = split_cols(jnp.einsum('btd,de->bte', h, w_in), ODD_SPLITS)
    qc = qc.reshape(B, T, C_HEADS, C_DH)
    qc_r = rope(qc, pos)
    kvc = kvc.reshape(B, T, 2, C_DH)
    kvs = kvs.reshape(B, T, 2, C_DH)
    kvs = jnp.stack([rope(kvs[:, :, 0], pos), kvs[:, :, 1]], axis=2)
    kvw = kvw.reshape(B, T, 2, C_DH)
    kvw = jnp.stack([rope(kvw[:, :, 0], pos), kvw[:, :, 1]], axis=2)
    gates = jax.nn.sigmoid(gc.astype(jnp.float32)).reshape(B, T, C_HEADS, 3).astype(h.dtype)
    qd = rope(qd.reshape(B, T, D_HEADS, 2, D_DH), pos)
    kd = rope(kd.reshape(B, T, D_KVH, 2, D_DH), pos).reshape(B, T, D_KVH, 2 * D_DH)
    kvd = jnp.stack([kd, vd.reshape(B, T, D_KVH, 2 * D_DH)], axis=2)
    if past is None:
        cmp_full, sel_full, win_full, d_full = kvc, kvs, kvw, kvd
        keep = min(WINDOW, T)
    else:
        cmp_full = jnp.concatenate([past[0], kvc], axis=1)
        sel_full = jnp.concatenate([past[1], kvs], axis=1)
        win_full = jnp.concatenate([past[2], kvw], axis=1)
        d_full = jnp.concatenate([past[3], kvd], axis=1)
        keep = past[2].shape[1]
    new_win = win_full[:, win_full.shape[1] - keep:]
    win_pos0 = past_len + T - win_full.shape[1]
    kcmp = compress_blocks(cmp_full[:, :, 0], phi_k, pe_k)
    vcmp = compress_blocks(cmp_full[:, :, 1], phi_v, pe_v)
    cmp_end = (jnp.arange(kcmp.shape[1], dtype=jnp.int32) + 1) * CMP_BLK - 1
    S = sel_full.shape[1]
    n_sel = -(-S // SEL_BLK)
    sel_pad = jnp.pad(sel_full, ((0, 0), (0, n_sel * SEL_BLK - S), (0, 0), (0, 0)))
    ks_blk = sel_pad[:, :, 0].reshape(B, n_sel, SEL_BLK, C_DH)
    vs_blk = sel_pad[:, :, 1].reshape(B, n_sel, SEL_BLK, C_DH)
    kw_pad = jnp.pad(win_full[:, :, 0], ((0, 0), (WINDOW, 0), (0, 0)))
    vw_pad = jnp.pad(win_full[:, :, 1], ((0, 0), (WINDOW, 0), (0, 0)))
    kw_pos = jnp.concatenate([jnp.full((WINDOW,), FAR_POS, jnp.int32), win_pos0 + jnp.arange(win_full.shape[1], dtype=jnp.int32)])
    lv = lam_vec.astype(jnp.float32)
    lam = jnp.exp(jnp.sum(lv[0] * lv[1])) - jnp.exp(jnp.sum(lv[2] * lv[3])) + lam_init
    kd_full, vd_full = d_full[:, :, 0], d_full[:, :, 1]

    def block(qc_b, qcr_b, g_b, qd_b, q_pos):
        oc = nsa_attend(qc_b, qcr_b, g_b, kcmp, vcmp, cmp_end, ks_blk, vs_blk, kw_pad, vw_pad, kw_pos, win_pos0, q_pos)
        od = diff_attend(qd_b, kd_full, vd_full, lam, lam_init, subln, q_pos)
        return jnp.concatenate([oc, od], axis=-1)

    out = sweep_queries(block, (qc, qc_r, gates, qd), pos)
    return out, (kvc, kvs, new_win, kvd)


def hier_moe(h, w_group, w_expert, w_gate, w_up, w_down):
    B, T, D = h.shape
    hf = h.reshape(B * T, D)
    g_logit = jnp.einsum('nd,dg->ng', hf, w_group, preferred_element_type=jnp.float32)
    grp = jnp.argmax(g_logit, axis=-1)
    g_w = jnp.take_along_axis(jax.nn.softmax(g_logit, axis=-1), grp[:, None], axis=-1)
    e_logit = jnp.einsum('nd,de->ne', hf, w_expert, preferred_element_type=jnp.float32).reshape(-1, N_GROUPS, E_PER_GROUP)
    e_in = jnp.take_along_axis(e_logit, grp[:, None, None], axis=1)[:, 0]
    top_v, top_i = lax.top_k(e_in, TOP_E)
    top_w = jax.nn.softmax(top_v, axis=-1) * g_w
    comb = jnp.sum(jax.nn.one_hot(grp[:, None] * E_PER_GROUP + top_i, N_EXPERTS, dtype=jnp.float32) * top_w[..., None], axis=1)
    act = jax.nn.silu(jnp.einsum('nd,edf->nef', hf, w_gate)) * jnp.einsum('nd,edf->nef', hf, w_up)
    out = jnp.einsum('nef,efd->nd', act * comb[..., None].astype(act.dtype), w_down)
    return out.reshape(B, T, D)


def trunk_layer(x, c, past, mixer, w_ada, b_ada, g_mix, g_ffn, w_out, w_group, w_expert, w_gate, w_up, w_down):
    B = x.shape[0]
    mod = (jnp.einsum('bd,de->be', jax.nn.silu(c), w_ada) + b_ada).reshape(B, 6, 1, D_MODEL)
    h = rms_norm(x, g_mix) * (1.0 + mod[:, 1]) + mod[:, 0]
    mix, rows = mixer(h, past)
    x = x + mod[:, 2] * jnp.einsum('bte,ed->btd', mix, w_out)
    h = rms_norm(x, g_ffn) * (1.0 + mod[:, 4]) + mod[:, 3]
    x = x + mod[:, 5] * hier_moe(h, w_group, w_expert, w_gate, w_up, w_down)
    return x, rows


def setup_inputs(seed: int = 0) -> dict:
    keys = iter(jax.random.split(jax.random.key(seed), 48))

    def nrm(shape, scale=1.0):
        return jax.random.normal(next(keys), shape, jnp.float32) * scale

    def gain(shape):
        return 1.0 + 0.01 * nrm(shape)

    n_pages = PAST_LEN // PAGE_SIZE
    n_pool = (DEC_BATCH * n_pages * 5) // 4
    win_buf = min(WINDOW, PAST_LEN)
    d_in = D_MODEL ** -0.5
    page_table = jax.random.permutation(next(keys), n_pool)[:DEC_BATCH * n_pages].reshape(DEC_BATCH, n_pages).astype(jnp.int32)
    return {
        'x_prompt': nrm((BATCH, SEQ, D_MODEL)),
        'x_sample': nrm((DEC_BATCH, DEC_SEQ, D_MODEL)),
        'c_prompt': nrm((BATCH, D_MODEL)),
        'c_sample': nrm((DEC_BATCH, D_MODEL)),
        'cache_a_kv': nrm((N_EVEN, n_pool, PAGE_SIZE, 2, A_KVH, A_DH)),
        'cache_a_idx': nrm((N_EVEN, n_pool, PAGE_SIZE, IDX_DH)),
        'cache_b_latent': nrm((N_EVEN, n_pool, PAGE_SIZE, B_KVLORA)),
        'cache_b_krope': nrm((N_EVEN, n_pool, PAGE_SIZE, B_ROPE)),
        'cache_c_cmp': nrm((N_ODD, n_pool, PAGE_SIZE, 2, C_DH)),
        'cache_c_sel': nrm((N_ODD, n_pool, PAGE_SIZE, 2, C_DH)),
        'state_c_win': nrm((N_ODD, DEC_BATCH, win_buf, 2, C_DH)),
        'cache_d_kv': nrm((N_ODD, n_pool, PAGE_SIZE, 2, D_KVH, 2 * D_DH)),
        'page_table': page_table,
        'w_ada': nrm((DEPTH, D_MODEL, 6 * D_MODEL), 0.5 * d_in),
        'b_ada': nrm((DEPTH, 6 * D_MODEL), 0.02),
        'g_mix': gain((DEPTH, D_MODEL)),
        'g_ffn': gain((DEPTH, D_MODEL)),
        'w_out': nrm((DEPTH, D_MODEL, D_MODEL), d_in),
        'w_in_even': nrm((N_EVEN, D_MODEL, EVEN_COLS), d_in),
        'b_q_norm': gain((N_EVEN, B_QLORA)),
        'b_kv_norm': gain((N_EVEN, B_KVLORA)),
        'w_b_qb': nrm((N_EVEN, B_QLORA, B_HEADS, B_NOPE + B_ROPE), B_QLORA ** -0.5),
        'w_b_kvb': nrm((N_EVEN, B_KVLORA, B_HEADS, B_NOPE + B_VDH), B_KVLORA ** -0.5),
        'w_in_odd': nrm((N_ODD, D_MODEL, ODD_COLS), d_in),
        'c_phi_k': nrm((N_ODD, CMP_BLK, C_DH, C_DH), (CMP_BLK * C_DH) ** -0.5),
        'c_phi_v': nrm((N_ODD, CMP_BLK, C_DH, C_DH), (CMP_BLK * C_DH) ** -0.5),
        'c_pe_k': nrm((N_ODD, CMP_BLK, C_DH), 0.1),
        'c_pe_v': nrm((N_ODD, CMP_BLK, C_DH), 0.1),
        'd_lambda': nrm((N_ODD, 4, D_DH), 0.1),
        'd_subln': gain((N_ODD, 2 * D_DH)),
        'moe_w_group': nrm((DEPTH, D_MODEL, N_GROUPS), d_in),
        'moe_w_expert': nrm((DEPTH, D_MODEL, N_EXPERTS), d_in),
        'moe_w_gate': nrm((DEPTH, N_EXPERTS, D_MODEL, E_HIDDEN), d_in),
        'moe_w_up': nrm((DEPTH, N_EXPERTS, D_MODEL, E_HIDDEN), d_in),
        'moe_w_down': nrm((DEPTH, N_EXPERTS, E_HIDDEN, D_MODEL), E_HIDDEN ** -0.5),
        'g_final': gain((D_MODEL,)),
    }


def reference(x_prompt, x_sample, c_prompt, c_sample, cache_a_kv, cache_a_idx, cache_b_latent, cache_b_krope,
              cache_c_cmp, cache_c_sel, state_c_win, cache_d_kv, page_table,
              w_ada, b_ada, g_mix, g_ffn, w_out, w_in_even, b_q_norm, b_kv_norm, w_b_qb, w_b_kvb,
              w_in_odd, c_phi_k, c_phi_v, c_pe_k, c_pe_v, d_lambda, d_subln,
              moe_w_group, moe_w_expert, moe_w_gate, moe_w_up, moe_w_down, g_final):
    xp, xs = x_prompt, x_sample
    even_p, even_s, odd_p, odd_s = [], [], [], []
    for l in range(DEPTH):
        li = l // 2
        if l % 2 == 0:
            mixer = functools.partial(even_mixer, w_in=w_in_even[li], q_norm=b_q_norm[li], kv_norm=b_kv_norm[li],
                                      w_qb=w_b_qb[li], w_kvb=w_b_kvb[li])
            past = (paged(cache_a_kv, li, page_table), paged(cache_a_idx, li, page_table),
                    paged(cache_b_latent, li, page_table), paged(cache_b_krope, li, page_table))
            rows_p, rows_s = even_p, even_s
        else:
            mixer = functools.partial(odd_mixer, w_in=w_in_odd[li], phi_k=c_phi_k[li], phi_v=c_phi_v[li],
                                      pe_k=c_pe_k[li], pe_v=c_pe_v[li], lam_vec=d_lambda[li], subln=d_subln[li],
                                      lam_init=0.8 - 0.6 * math.exp(-0.3 * l))
            past = (paged(cache_c_cmp, li, page_table), paged(cache_c_sel, li, page_table),
                    state_c_win[li], paged(cache_d_kv, li, page_table))
            rows_p, rows_s = odd_p, odd_s
        layer = functools.partial(trunk_layer, mixer=mixer, w_ada=w_ada[l], b_ada=b_ada[l], g_mix=g_mix[l],
                                  g_ffn=g_ffn[l], w_out=w_out[l], w_group=moe_w_group[l], w_expert=moe_w_expert[l],
                                  w_gate=moe_w_gate[l], w_up=moe_w_up[l], w_down=moe_w_down[l])
        xp, rp = layer(xp, c_prompt, None)
        xs, rs = layer(xs, c_sample, past)
        rows_p.append(rp)
        rows_s.append(rs)
    stack = lambda rows, i: jnp.stack([r[i] for r in rows])
    y_prompt = rms_norm(xp, g_final)
    y_sample = rms_norm(xs, g_final)
    return (y_prompt, y_sample,
            stack(even_p, 0), stack(even_s, 0), stack(even_p, 1), stack(even_s, 1),
            stack(even_p, 2), stack(even_s, 2), stack(even_p, 3), stack(even_s, 3),
            stack(odd_p, 0), stack(odd_s, 0), stack(odd_p, 1), stack(odd_s, 1),
            stack(odd_p, 2), stack(odd_s, 2), stack(odd_p, 3), stack(odd_s, 3))
```

```python
import functools
import math

import jax
import jax.numpy as jnp
import numpy as np
from jax import lax
from jax.experimental import pallas as pl
from jax.experimental.pallas import tpu as pltpu

F32 = jnp.float32
BF16 = jnp.bfloat16
I32 = jnp.int32

D_MODEL = 1024
PAGE_SIZE = 128
QBLK = 128
ROPE_THETA = 10000.0
EPS = 1e-6
NEG = -1e30
FORCE_SCORE = 1e9

A_HEADS, A_DH, A_KVH = 8, 64, 2
IDX_HEADS, IDX_DH, IDX_TOPK_MAX = 8, 64, 256
B_HEADS, B_NOPE, B_ROPE, B_VDH, B_QLORA, B_KVLORA = 8, 64, 32, 64, 256, 128
C_HEADS, C_DH, CMP_BLK, SEL_BLK, N_SEL, WINDOW = 8, 64, 32, 64, 16, 512
D_HEADS, D_DH, D_KVH = 4, 64, 2
N_GROUPS, E_PER_GROUP, E_HIDDEN = 4, 4, 256
N_EXPERTS = N_GROUPS * E_PER_GROUP

LANES = 128
KEY_CHUNK = 512
GATHER_PAGES = 8
VMEM_LIMIT = 56 * 1024 * 1024

EV_QA, EV_QI, EV_AKV, EV_KI, EV_CKV, EV_QLAT, EV_KR = 0, 512, 1024, 1280, 1408, 1536, 1792
EV_WI = EV_KR + B_ROPE
EV_OUT = 1920
OD_QC, OD_QCR, OD_QD, OD_DKV, OD_CMP, OD_SEL, OD_WIN, OD_GATE = 0, 512, 1024, 1536, 2048, 2176, 2304, 2432
OD_OUT = 2560


def _dot(a, b):
    return jnp.dot(a, b, preferred_element_type=F32)


def _dot_nt(a, b):
    return lax.dot_general(a, b, (((1,), (1,)), ((), ())), preferred_element_type=F32)


def _params(*sem):
    return pltpu.CompilerParams(dimension_semantics=sem, vmem_limit_bytes=VMEM_LIMIT)


def _rms(x, g):
    return x * lax.rsqrt(jnp.mean(x * x, axis=-1, keepdims=True) + EPS) * g


def _sigmoid(x):
    return 1.0 / (1.0 + jnp.exp(-x))


def _ada_kernel(c_ref, w_ref, b_ref, o_ref):
    c = c_ref[...]
    s = (c * _sigmoid(c)).astype(BF16)
    o_ref[0] = _dot(s, w_ref[0].astype(BF16)) + b_ref[0]


def _ada(c_all, w_ada, b_ada):
    depth, d, n = w_ada.shape
    bc = c_all.shape[0]
    nb = 1536
    return pl.pallas_call(
        _ada_kernel,
        grid=(depth, n // nb),
        in_specs=[pl.BlockSpec((bc, d), lambda l, j: (0, 0)),
                  pl.BlockSpec((1, d, nb), lambda l, j: (l, 0, j)),
                  pl.BlockSpec((1, 1, nb), lambda l, j: (l, 0, j))],
        out_specs=pl.BlockSpec((1, bc, nb), lambda l, j: (l, 0, j)),
        out_shape=jax.ShapeDtypeStruct((depth, bc, n), F32),
        compiler_params=_params("arbitrary", "arbitrary"), name="ada",
    )(c_all, w_ada, b_ada.reshape(depth, 1, n))


def _rope_tables(pos, width, active):
    half = width // 2
    inv = ROPE_THETA ** (-jnp.arange(half, dtype=F32) / half)
    ang = pos.astype(F32)[:, None] * inv[None, :]
    cos, sin = jnp.cos(ang), jnp.sin(ang)
    lane = np.arange(LANES)
    j = lane % width
    cos_l = cos[:, j % half]
    sin_l = sin[:, j % half] * jnp.asarray(np.where(j < half, -1.0, 1.0), F32)[None, :]
    act = jnp.asarray(lane < active)[None, :]
    return jnp.where(act, cos_l, 1.0), jnp.where(act, sin_l, 0.0)


def _all_tables(pos):
    t = []
    for width, active in ((64, 128), (64, 64), (32, 32)):
        t.extend(_rope_tables(pos, width, active))
    return jnp.stack(t)


TAB = {"rope64": 0, "rope64h": 2, "rope32": 4}
WIDTH = {"rope64": 64, "rope64h": 64, "rope32": 32}


def _rope(y, cos, sin, width):
    half = width // 2
    lane = lax.broadcasted_iota(I32, y.shape, 1)
    first = (lane & (width - 1)) < half
    up = pltpu.roll(y, LANES - half, axis=1)
    dn = pltpu.roll(y, half, axis=1)
    return y * cos + jnp.where(first, up, dn) * sin


def _proj_kernel(x_ref, sh_ref, sc_ref, g_ref, w_ref, tab_ref, kvn_ref, o_ref, *, chunks, groups):
    bb, tt, d = x_ref.shape
    h = _rms(x_ref[...], g_ref[...]) * (1.0 + sc_ref[...]) + sh_ref[...]
    hb = h.reshape(bb * tt, d).astype(BF16)
    for c0, cw in chunks:
        y = _dot(hb, w_ref[:, c0:c0 + cw])
        for k in range(cw // LANES):
            yg = y[:, k * LANES:(k + 1) * LANES]
            for kind, oc in groups[c0 // LANES + k]:
                if kind == "plain":
                    v = yg
                elif kind == "rms":
                    v = _rms(yg, kvn_ref[...])
                elif kind == "sigmoid":
                    v = _sigmoid(yg)
                else:
                    v = _rope(yg, tab_ref[TAB[kind]], tab_ref[TAB[kind] + 1], WIDTH[kind])
                o_ref[:, :, oc:oc + LANES] = v.reshape(bb, tt, LANES)


def _proj(x, shift, scale, g, w, tabs, kvn, groups, n_out, bb, tt):
    bx, tx, d = x.shape
    nw = w.shape[1]
    chunks = [(c0, min(256, nw - c0)) for c0 in range(0, nw, 256)]
    rows = bb * tt
    tab_map = (lambda i, j: (0, j, 0)) if bb == 1 else (lambda i, j: (0, 0, 0))
    return pl.pallas_call(
        functools.partial(_proj_kernel, chunks=chunks, groups=groups),
        grid=(bx // bb, tx // tt),
        in_specs=[pl.BlockSpec((bb, tt, d), lambda i, j: (i, j, 0)),
                  pl.BlockSpec((bb, 1, d), lambda i, j: (i, 0, 0)),
                  pl.BlockSpec((bb, 1, d), lambda i, j: (i, 0, 0)),
                  pl.BlockSpec((1, 1, d), lambda i, j: (0, 0, 0)),
                  pl.BlockSpec((d, nw), lambda i, j: (0, 0)),
                  pl.BlockSpec((6, rows, LANES), tab_map),
                  pl.BlockSpec((1, LANES), lambda i, j: (0, 0))],
        out_specs=pl.BlockSpec((bb, tt, n_out), lambda i, j: (i, j, 0)),
        out_shape=jax.ShapeDtypeStruct((bx, tx, n_out), F32),
        compiler_params=_params("arbitrary", "arbitrary"), name="proj",
    )(x, shift, scale, g.reshape(1, 1, d), w, tabs, kvn)


def _even_weights(w_in):
    d = w_in.shape[0]
    qa, ka, va, qi, wi, ki, qlat, kvlat = jnp.split(
        w_in, [512, 640, 768, 1280, 1288, 1352, 1608], axis=1)
    z = lambda n: jnp.zeros((d, n), w_in.dtype)
    w = jnp.concatenate([qa, qi, ka, va, ki, z(64), kvlat[:, :B_KVLORA], qlat,
                         kvlat[:, B_KVLORA:], wi, z(LANES - B_ROPE - IDX_HEADS)], axis=1)
    groups = ([[("rope64", EV_QA + LANES * k)] for k in range(4)]
              + [[("rope64", EV_QI + LANES * k)] for k in range(4)]
              + [[("rope64", EV_AKV)], [("plain", EV_AKV + LANES)], [("rope64h", EV_KI)],
                 [("rms", EV_CKV)], [("plain", EV_QLAT)], [("plain", EV_QLAT + LANES)],
                 [("rope32", EV_KR)]])
    return w.astype(BF16), groups


def _odd_weights(w_in):
    d = w_in.shape[0]
    qc, kvc, kvs, kvw, gc, qd, kd, vd = jnp.split(
        w_in, [512, 640, 768, 896, 920, 1432, 1688], axis=1)
    w = jnp.concatenate([qc, qd, kd, vd, kvc, kvs, kvw, gc,
                         jnp.zeros((d, LANES - 3 * C_HEADS), w_in.dtype)], axis=1)
    groups = ([[("plain", OD_QC + LANES * k), ("rope64", OD_QCR + LANES * k)] for k in range(4)]
              + [[("rope64", OD_QD + LANES * k)] for k in range(4)]
              + [[("rope64", OD_DKV)], [("rope64", OD_DKV + LANES)],
                 [("plain", OD_DKV + 2 * LANES)], [("plain", OD_DKV + 3 * LANES)],
                 [("plain", OD_CMP)], [("rope64h", OD_SEL)], [("rope64h", OD_WIN)],
                 [("sigmoid", OD_GATE)]])
    return w.astype(BF16), groups


def _gather_kernel(pt_ref, *refs, n_pools, widths, offs):
    g = GATHER_PAGES
    pages = refs[:n_pools * g]
    news = refs[n_pools * g:n_pools * g + n_pools]
    o_ref = refs[-1]
    j = pl.program_id(1)
    last = pl.num_programs(1) - 1
    o_ref[...] = jnp.zeros(o_ref.shape, o_ref.dtype)

    @pl.when(j < last)
    def _():
        for p in range(n_pools):
            for k in range(g):
                o_ref[0, k * PAGE_SIZE:(k + 1) * PAGE_SIZE, offs[p]:offs[p] + widths[p]] = (
                    pages[p * g + k][0, 0].astype(BF16))

    @pl.when(j == last)
    def _():
        for p in range(n_pools):
            tn = news[p].shape[1]
            o_ref[0, 0:tn, offs[p]:offs[p] + widths[p]] = news[p][0].astype(BF16)


def _gather(page_table, pools, li, news, f_tot):
    g = GATHER_PAGES
    b, n_pages = page_table.shape
    assert n_pages % g == 0
    nsteps = n_pages // g + 1
    widths = [p.shape[-1] for p in pools]
    offs = list(np.cumsum([0] + widths[:-1]))
    in_specs = []
    args = []
    for p in pools:
        for k in range(g):
            in_specs.append(pl.BlockSpec(
                (1, 1, PAGE_SIZE, p.shape[-1]),
                lambda i, j, pt, k=k: (li, pt[i, jnp.minimum(j, nsteps - 2) * g + k], 0, 0)))
            args.append(p)
    for nw in news:
        in_specs.append(pl.BlockSpec((1,) + nw.shape[1:], lambda i, j, pt: (i, 0, 0)))
        args.append(nw)
    rows = g * PAGE_SIZE
    return pl.pallas_call(
        functools.partial(_gather_kernel, n_pools=len(pools), widths=widths, offs=offs),
        grid_spec=pltpu.PrefetchScalarGridSpec(
            num_scalar_prefetch=1, grid=(b, nsteps), in_specs=in_specs,
            out_specs=pl.BlockSpec((1, rows, f_tot), lambda i, j, pt: (i, j, 0))),
        out_shape=jax.ShapeDtypeStruct((b, nsteps * rows, f_tot), BF16),
        compiler_params=_params("arbitrary", "arbitrary"), name="gather",
    )(page_table, *args)


def _online_update(s, mask, v, m_ref, l_ref, acc_ref, idx):
    sm = jnp.where(mask, s, NEG)
    m_old = m_ref[idx]
    m_new = jnp.maximum(m_old, jnp.max(sm, axis=1, keepdims=True))
    p = jnp.where(mask, jnp.exp(sm - m_new), 0.0)
    alpha = jnp.exp(m_old - m_new)
    l_ref[idx] = alpha * l_ref[idx] + jnp.sum(p, axis=1, keepdims=True)
    acc_ref[idx] = alpha * acc_ref[idx] + _dot(p.astype(BF16), v)
    m_ref[idx] = m_new


def _init_state(m_ref, l_ref, acc_ref):
    m_ref[...] = jnp.full(m_ref.shape, NEG, F32)
    l_ref[...] = jnp.zeros(l_ref.shape, F32)
    acc_ref[...] = jnp.zeros(acc_ref.shape, F32)


def _row_pos(rows, tq, q0):
    r = lax.broadcasted_iota(I32, (rows, 1), 0)
    return q0 + (r & (tq - 1))


def _sort_key(x):
    b = pltpu.bitcast(x + 0.0, I32)
    return b ^ ((b >> 31) & 0x7FFFFFFF)


def _count(load, nchunks, width, rows, pred):
    lane = lax.broadcasted_iota(I32, (rows, LANES), 1)

    def body(c, acc):
        k = load(c)
        for j in range(width // LANES):
            idx = c * width + j * LANES + lane
            acc = acc + jnp.where(pred(k[:, j * LANES:(j + 1) * LANES], idx), 1.0, 0.0)
        return acc

    zero = jnp.zeros((rows, LANES), F32)
    acc = body(0, zero) if isinstance(nchunks, int) and nchunks == 1 else lax.fori_loop(0, nchunks, body, zero)
    return jnp.sum(acc, axis=1, keepdims=True)


def _topk_threshold(load, nchunks, width, rows, k, idx_bits):
    kf = float(k)
    int_min = jnp.int32(-2 ** 31)

    def bit_step(it, t):
        cand = t ^ lax.shift_left(jnp.int32(1), jnp.int32(31) - it)
        cnt = _count(load, nchunks, width, rows, lambda key, idx: key >= cand)
        return jnp.where(cnt >= kf, cand, t)

    t = lax.fori_loop(0, 32, bit_step, jnp.full((rows, 1), int_min, I32))
    need = kf - _count(load, nchunks, width, rows, lambda key, idx: key > t)

    def idx_step(it, p):
        cand = p | lax.shift_left(jnp.int32(1), jnp.int32(idx_bits - 1) - it)
        cnt = _count(load, nchunks, width, rows, lambda key, idx: (key == t) & (idx < cand))
        return jnp.where(cnt < need, cand, p)

    j = lax.fori_loop(0, idx_bits, idx_step, jnp.zeros((rows, 1), I32))
    return t, j


def _selected(key, idx, t, j):
    return jnp.where(key > t, 1.0, jnp.where(key == t, jnp.where(idx <= j, 1.0, 0.0), 0.0))


def _tile_rows(x, n):
    return jnp.concatenate([x] * n, axis=0)


def _dsa_kernel(qi_ref, wi_ref, qa_ref, ki_ref, kv_ref, o_ref, key_ref, m_ref, l_ref, acc_ref,
                *, tq, past, k_top, idx_bits):
    ch = KEY_CHUNK
    q0 = past + pl.program_id(1) * tq
    nc = (q0 + tq - 1) // ch + 1
    qpos = q0 + lax.broadcasted_iota(I32, (tq, 1), 0)
    lane = lax.broadcasted_iota(I32, (tq, ch), 1)
    qi = qi_ref[0, 0]
    wi = wi_ref[0]

    def score_step(c, carry):
        off = pl.multiple_of(c * ch, ch)
        s = _dot_nt(qi, ki_ref[0, pl.ds(off, ch), :])
        acc = jnp.zeros((tq, ch), F32)
        for h in range(IDX_HEADS):
            acc = acc + jnp.maximum(s[h * tq:(h + 1) * tq], 0.0) * wi[:, h:h + 1]
        score = jnp.where(off + lane <= qpos, acc, -jnp.inf)
        key_ref[:, pl.ds(off, ch)] = _sort_key(score)
        return carry

    lax.fori_loop(0, nc, score_step, 0)
    load = lambda c: key_ref[:, pl.ds(pl.multiple_of(c * ch, ch), ch)]
    t, j = _topk_threshold(load, nc, ch, tq, k_top, idx_bits)

    _init_state(m_ref, l_ref, acc_ref)
    rep = A_HEADS // A_KVH

    def attend_step(c, carry):
        off = pl.multiple_of(c * ch, ch)
        idx = off + lane
        sel = jnp.where(idx <= qpos, _selected(key_ref[:, pl.ds(off, ch)], idx, t, j), 0.0)
        mask = _tile_rows(sel, rep) > 0.0
        kv = kv_ref[0, pl.ds(off, ch), :]
        kk, vv = kv[:, :LANES], kv[:, LANES:]
        for g in range(A_KVH):
            s = _dot_nt(qa_ref[0, 0, g * rep * tq:(g + 1) * rep * tq, :], kk)
            _online_update(s, mask, vv, m_ref, l_ref, acc_ref, g)
        return carry

    lax.fori_loop(0, nc, attend_step, 0)
    for g in range(A_KVH):
        o = acc_ref[g] / jnp.maximum(l_ref[g], 1e-30)
        o_ref[0, 0, g * rep * tq:(g + 1) * rep * tq, :] = o.astype(BF16)


def _dsa(qi, wi, qa, ki, kv, tq, past, s_total):
    b, nq, rows, _ = qi.shape
    s_pad = ki.shape[1]
    k_top = min(IDX_TOPK_MAX, s_total // 4)
    rep_rows = rows // A_KVH
    return pl.pallas_call(
        functools.partial(_dsa_kernel, tq=tq, past=past, k_top=k_top,
                          idx_bits=int(math.ceil(math.log2(s_pad)))),
        grid=(b, nq),
        in_specs=[pl.BlockSpec((1, 1, rows, IDX_DH), lambda i, j: (i, j, 0, 0)),
                  pl.BlockSpec((1, tq, IDX_HEADS), lambda i, j: (i, j, 0)),
                  pl.BlockSpec((1, 1, rows, LANES), lambda i, j: (i, j, 0, 0)),
                  pl.BlockSpec((1, s_pad, IDX_DH), lambda i, j: (i, 0, 0)),
                  pl.BlockSpec((1, s_pad, 2 * LANES), lambda i, j: (i, 0, 0))],
        out_specs=pl.BlockSpec((1, 1, rows, LANES), lambda i, j: (i, j, 0, 0)),
        out_shape=jax.ShapeDtypeStruct((b, nq, rows, LANES), BF16),
        scratch_shapes=[pltpu.VMEM((tq, s_pad), I32),
                        pltpu.VMEM((A_KVH, rep_rows, 1), F32),
                        pltpu.VMEM((A_KVH, rep_rows, 1), F32),
                        pltpu.VMEM((A_KVH, rep_rows, LANES), F32)],
        compiler_params=_params("arbitrary", "arbitrary"), name="dsa",
    )(qi, wi, qa, ki, kv)


def _mlaq_kernel(ql_ref, qn_ref, wn_ref, wr_ref, wuk_ref, tab_ref, o_ref):
    scale = (B_NOPE + B_ROPE) ** -0.5
    rb = _rms(ql_ref[...], qn_ref[...]).astype(BF16)
    qn_all = _dot(rb, wn_ref[...])
    qr_all = _dot(rb, wr_ref[...])
    for h in range(B_HEADS):
        sl = slice(h * LANES, (h + 1) * LANES)
        q_lat = _dot(qn_all[:, sl].astype(BF16), wuk_ref[h])
        q_rope = _rope(qr_all[:, sl], tab_ref[TAB["rope32"]], tab_ref[TAB["rope32"] + 1], B_ROPE)
        o_ref[0, h, :, 0:LANES] = (q_lat * scale).astype(BF16)
        o_ref[0, h, :, LANES:2 * LANES] = (q_rope * scale).astype(BF16)


def _mlaq(qlat, q_norm, wn, wr, wuk, tabs, tab_blocks):
    n = qlat.shape[0]
    nb = n // QBLK
    tab_map = lambda i: (0, i % tab_blocks, 0)
    return pl.pallas_call(
        _mlaq_kernel,
        grid=(nb,),
        in_specs=[pl.BlockSpec((QBLK, B_QLORA), lambda i: (i, 0)),
                  pl.BlockSpec((1, B_QLORA), lambda i: (0, 0)),
                  pl.BlockSpec((B_QLORA, B_HEADS * LANES), lambda i: (0, 0)),
                  pl.BlockSpec((B_QLORA, B_HEADS * LANES), lambda i: (0, 0)),
                  pl.BlockSpec((B_HEADS, LANES, LANES), lambda i: (0, 0, 0)),
                  pl.BlockSpec((6, QBLK, LANES), tab_map)],
        out_specs=pl.BlockSpec((1, B_HEADS, QBLK, 2 * LANES), lambda i: (i, 0, 0, 0)),
        out_shape=jax.ShapeDtypeStruct((nb, B_HEADS, QBLK, 2 * LANES), BF16),
        compiler_params=_params("arbitrary"), name="mlaq",
    )(qlat, q_norm, wn, wr, wuk, tabs)


def _mla_kernel(q_ref, kv_ref, o_ref, m_ref, l_ref, acc_ref, *, tq, past):
    ch = KEY_CHUNK
    rows = q_ref.shape[2]
    q0 = past + pl.program_id(1) * tq
    nc = (q0 + tq - 1) // ch + 1
    qpos = _row_pos(rows, tq, q0)
    lane = lax.broadcasted_iota(I32, (1, ch), 1)
    q = q_ref[0, 0]
    _init_state(m_ref, l_ref, acc_ref)

    def step(c, carry):
        off = pl.multiple_of(c * ch, ch)
        kc = kv_ref[0, pl.ds(off, ch), :]
        _online_update(_dot_nt(q, kc), off + lane <= qpos, kc[:, :LANES], m_ref, l_ref, acc_ref, 0)
        return carry

    lax.fori_loop(0, nc, step, 0)
    o_ref[0, 0] = (acc_ref[0] / jnp.maximum(l_ref[0], 1e-30)).astype(BF16)


def _mla(q, kv, tq, past):
    b, nq, rows, f = q.shape
    s_pad = kv.shape[1]
    return pl.pallas_call(
        functools.partial(_mla_kernel, tq=tq, past=past),
        grid=(b, nq),
        in_specs=[pl.BlockSpec((1, 1, rows, f), lambda i, j: (i, j, 0, 0)),
                  pl.BlockSpec((1, s_pad, f), lambda i, j: (i, 0, 0))],
        out_specs=pl.BlockSpec((1, 1, rows, LANES), lambda i, j: (i, j, 0, 0)),
        out_shape=jax.ShapeDtypeStruct((b, nq, rows, LANES), BF16),
        scratch_shapes=[pltpu.VMEM((1, rows, 1), F32), pltpu.VMEM((1, rows, 1), F32),
                        pltpu.VMEM((1, rows, LANES), F32)],
        compiler_params=_params("arbitrary", "arbitrary"), name="mla",
    )(q, kv)


def _mlao_kernel(ol_ref, wuv_ref, o_ref):
    for jp in range(B_HEADS // 2):
        o = _dot(ol_ref[0, 2 * jp], wuv_ref[2 * jp]) + _dot(ol_ref[0, 2 * jp + 1], wuv_ref[2 * jp + 1])
        o_ref[:, jp * LANES:(jp + 1) * LANES] = o.astype(BF16)


def _mlao(olat, wuv):
    nb = olat.shape[0]
    return pl.pallas_call(
        _mlao_kernel,
        grid=(nb,),
        in_specs=[pl.BlockSpec((1, B_HEADS, QBLK, LANES), lambda i: (i, 0, 0, 0)),
                  pl.BlockSpec((B_HEADS, LANES, LANES), lambda i: (0, 0, 0))],
        out_specs=pl.BlockSpec((QBLK, B_HEADS * B_VDH), lambda i: (i, 0)),
        out_shape=jax.ShapeDtypeStruct((nb * QBLK, B_HEADS * B_VDH), BF16),
        compiler_params=_params("arbitrary"), name="mlao",
    )(olat, wuv)


def _compress_kernel(x_ref, pe_ref, phi_ref, o_ref):
    phi = phi_ref[...]
    bias = _dot(pe_ref[...], phi)
    o_ref[0] = _dot(x_ref[0], phi) + bias[0:1]


def _compress(x, pe_rows, phi_big, n_blk):
    b = x.shape[0]
    f = x.shape[2]
    return pl.pallas_call(
        _compress_kernel,
        grid=(b,),
        in_specs=[pl.BlockSpec((1, n_blk, f), lambda i: (i, 0, 0)),
                  pl.BlockSpec((8, f), lambda i: (0, 0)),
                  pl.BlockSpec((f, LANES), lambda i: (0, 0))],
        out_specs=pl.BlockSpec((1, n_blk, LANES), lambda i: (i, 0, 0)),
        out_shape=jax.ShapeDtypeStruct((b, n_blk, LANES), F32),
        compiler_params=_params("arbitrary"), name="compress",
    )(x, pe_rows, phi_big)


def _nsa_kernel(qc_ref, qr_ref, gate_ref, cmp_ref, sel_ref, win_ref, o_ref, m_ref, l_ref, acc_ref,
                *, tq, past, win_base, wl, nsel_pad, n_top):
    ch = KEY_CHUNK
    heads = C_HEADS
    rows = heads * tq
    q0 = past + pl.program_id(1) * tq
    nc = (q0 + tq - 1) // ch + 1
    qpos_t = q0 + lax.broadcasted_iota(I32, (tq, 1), 0)
    qpos_r = _row_pos(rows, tq, q0)
    qc = qc_ref[0, 0]
    qr = qr_ref[0, 0]

    kvc = cmp_ref[0]
    nb = kvc.shape[0]
    nbh = nb // 2
    pcol = lax.broadcasted_iota(I32, (1, nb), 1)
    blk = jnp.where(pcol < nbh, 2 * pcol, 2 * (pcol - nbh) + 1)
    cmask = (blk + 1) * CMP_BLK - 1 <= qpos_r
    s_c = jnp.where(cmask, _dot_nt(qc, kvc), NEG)
    e_c = jnp.where(cmask, jnp.exp(s_c - jnp.max(s_c, axis=1, keepdims=True)), 0.0)
    p_c = e_c / jnp.maximum(jnp.sum(e_c, axis=1, keepdims=True), 1e-30)
    o_c = _dot(p_c.astype(BF16), kvc)

    imp = p_c[0:tq]
    for h in range(1, heads):
        imp = imp + p_c[h * tq:(h + 1) * tq]
    imp = imp[:, :nbh] + imp[:, nbh:]
    if nsel_pad > nbh:
        imp = jnp.concatenate([imp, jnp.zeros((tq, nsel_pad - nbh), F32)], axis=1)
    bidx = lax.broadcasted_iota(I32, (tq, nsel_pad), 1)
    forced = (bidx == (qpos_t >> 6)) | (bidx == 0)
    visible = bidx * SEL_BLK <= qpos_t
    imp = jnp.where(visible, jnp.where(forced, FORCE_SCORE, imp), -jnp.inf)
    keys = _sort_key(imp)
    t, j = _topk_threshold(lambda c: keys, 1, nsel_pad, tq, n_top, int(math.log2(nsel_pad)))
    bsel = _selected(keys, bidx, t, j).astype(BF16)

    _init_state(m_ref, l_ref, acc_ref)
    lane = lax.broadcasted_iota(I32, (1, ch), 1)
    erow = lax.broadcasted_iota(I32, (nsel_pad, ch), 0)
    ecol = lax.broadcasted_iota(I32, (nsel_pad, ch), 1) >> 6

    def step(c, carry):
        off = pl.multiple_of(c * ch, ch)
        expand = jnp.where(erow == c * (ch // SEL_BLK) + ecol, 1.0, 0.0).astype(BF16)
        ksel = _tile_rows(_dot(bsel, expand), heads)
        mask = (ksel > 0.5) & (off + lane <= qpos_r)
        kv = sel_ref[0, pl.ds(off, ch), :]
        _online_update(_dot_nt(qr, kv), mask, kv, m_ref, l_ref, acc_ref, 0)
        return carry

    lax.fori_loop(0, nc, step, 0)
    o_s = acc_ref[0] / jnp.maximum(l_ref[0], 1e-30)

    w_rows = win_ref.shape[1]
    start = jnp.clip(q0 - WINDOW - win_base, 0, w_rows - wl)
    start = pl.multiple_of(start, LANES)
    kw = win_ref[0, pl.ds(start, wl), :]
    dist = qpos_r - (win_base + start + lax.broadcasted_iota(I32, (1, wl), 1))
    wmask = (dist >= 0) & (dist < WINDOW)
    s_w = jnp.where(wmask, _dot_nt(qr, kw), NEG)
    e_w = jnp.where(wmask, jnp.exp(s_w - jnp.max(s_w, axis=1, keepdims=True)), 0.0)
    p_w = e_w / jnp.maximum(jnp.sum(e_w, axis=1, keepdims=True), 1e-30)
    o_w = _dot(p_w.astype(BF16), kw)

    gates = gate_ref[0]
    col = lambda k: jnp.concatenate([gates[:, 3 * h + k:3 * h + k + 1] for h in range(heads)], axis=0)
    o_ref[0, 0] = (col(0) * o_c + col(1) * o_s + col(2) * o_w).astype(BF16)


def _nsa(qc, qr, gates, kvcmp, kvsel, kvwin, tq, past, win_base, s_total):
    b, nq, rows, _ = qc.shape
    nb = kvcmp.shape[1]
    s_pad = kvsel.shape[1]
    w_rows = kvwin.shape[1]
    wl = min(w_rows, -(-(WINDOW + tq) // LANES) * LANES)
    n_sel = -(-s_total // SEL_BLK)
    nsel_pad = -(-n_sel // LANES) * LANES
    return pl.pallas_call(
        functools.partial(_nsa_kernel, tq=tq, past=past, win_base=win_base, wl=wl,
                          nsel_pad=nsel_pad, n_top=min(N_SEL, n_sel)),
        grid=(b, nq),
        in_specs=[pl.BlockSpec((1, 1, rows, LANES), lambda i, j: (i, j, 0, 0)),
                  pl.BlockSpec((1, 1, rows, LANES), lambda i, j: (i, j, 0, 0)),
                  pl.BlockSpec((1, tq, LANES), lambda i, j: (i, j, 0)),
                  pl.BlockSpec((1, nb, LANES), lambda i, j: (i, 0, 0)),
                  pl.BlockSpec((1, s_pad, LANES), lambda i, j: (i, 0, 0)),
                  pl.BlockSpec((1, w_rows, LANES), lambda i, j: (i, 0, 0))],
        out_specs=pl.BlockSpec((1, 1, rows, LANES), lambda i, j: (i, j, 0, 0)),
        out_shape=jax.ShapeDtypeStruct((b, nq, rows, LANES), BF16),
        scratch_shapes=[pltpu.VMEM((1, rows, 1), F32), pltpu.VMEM((1, rows, 1), F32),
                        pltpu.VMEM((1, rows, LANES), F32)],
        compiler_params=_params("arbitrary", "arbitrary"), name="nsa",
    )(qc, qr, gates, kvcmp, kvsel, kvwin)


def _diff_kernel(q_ref, kv_ref, lam_ref, sub_ref, o_ref, m_ref, l_ref, acc_ref, *, tq, past, lam_init):
    ch = KEY_CHUNK
    rep = D_HEADS // D_KVH
    rows = rep * tq
    q0 = past + pl.program_id(1) * tq
    nc = (q0 + tq - 1) // ch + 1
    qpos = _row_pos(rows, tq, q0)
    lane = lax.broadcasted_iota(I32, (1, ch), 1)
    _init_state(m_ref, l_ref, acc_ref)

    def step(c, carry):
        off = pl.multiple_of(c * ch, ch)
        mask = off + lane <= qpos
        kv = kv_ref[0, pl.ds(off, ch), :]
        for g in range(D_KVH):
            kk = kv[:, g * LANES:(g + 1) * LANES]
            vv = kv[:, (D_KVH + g) * LANES:(D_KVH + g + 1) * LANES]
            for m in range(2):
                idx = g * 2 + m
                s = _dot_nt(q_ref[0, 0, idx * rows:(idx + 1) * rows, :], kk)
                _online_update(s, mask, vv, m_ref, l_ref, acc_ref, idx)
        return carry

    lax.fori_loop(0, nc, step, 0)
    lv = lam_ref[...]
    lam = (jnp.exp(jnp.sum(lv[0:1] * lv[1:2], axis=1, keepdims=True))
           - jnp.exp(jnp.sum(lv[2:3] * lv[3:4], axis=1, keepdims=True)) + lam_init)
    for g in range(D_KVH):
        o = (acc_ref[2 * g] / jnp.maximum(l_ref[2 * g], 1e-30)
             - lam * (acc_ref[2 * g + 1] / jnp.maximum(l_ref[2 * g + 1], 1e-30)))
        o = _rms(o, sub_ref[...]) * (1.0 - lam_init)
        for r in range(rep):
            head = g * rep + r
            o_ref[0, :, head * LANES:(head + 1) * LANES] = o[r * tq:(r + 1) * tq]


def _diff(q, kv, lam_vec, subln, tq, past, lam_init):
    b, nq, qrows, _ = q.shape
    s_pad = kv.shape[1]
    rows = (D_HEADS // D_KVH) * tq
    return pl.pallas_call(
        functools.partial(_diff_kernel, tq=tq, past=past, lam_init=lam_init),
        grid=(b, nq),
        in_specs=[pl.BlockSpec((1, 1, qrows, LANES), lambda i, j: (i, j, 0, 0)),
                  pl.BlockSpec((1, s_pad, 4 * LANES), lambda i, j: (i, 0, 0)),
                  pl.BlockSpec((4, D_DH), lambda i, j: (0, 0)),
                  pl.BlockSpec((1, 2 * D_DH), lambda i, j: (0, 0))],
        out_specs=pl.BlockSpec((1, tq, D_HEADS * 2 * D_DH), lambda i, j: (i, j, 0)),
        out_shape=jax.ShapeDtypeStruct((b, nq * tq, D_HEADS * 2 * D_DH), F32),
        scratch_shapes=[pltpu.VMEM((2 * D_KVH, rows, 1), F32), pltpu.VMEM((2 * D_KVH, rows, 1), F32),
                        pltpu.VMEM((2 * D_KVH, rows, LANES), F32)],
        compiler_params=_params("arbitrary", "arbitrary"), name="diff",
    )(q, kv, lam_vec, subln)


def _route(logits):
    lane = lax.broadcasted_iota(I32, logits.shape, 1)
    lane_f = lane.astype(F32)
    big = 1e9
    is_g = lane < N_GROUPS
    gl = jnp.where(is_g, logits, -jnp.inf)
    gmax = jnp.max(gl, axis=1, keepdims=True)
    grp = jnp.min(jnp.where(gl == gmax, lane_f, big), axis=1, keepdims=True)
    g_w = 1.0 / jnp.sum(jnp.where(is_g, jnp.exp(gl - gmax), 0.0), axis=1, keepdims=True)
    e_grp = ((lane - N_GROUPS) >> 2).astype(F32)
    el = jnp.where(e_grp == grp, logits, -jnp.inf)
    el = jnp.where(lane >= N_GROUPS, jnp.where(lane < N_GROUPS + N_EXPERTS, el, -jnp.inf), -jnp.inf)
    v1 = jnp.max(el, axis=1, keepdims=True)
    i1 = jnp.min(jnp.where(el == v1, lane_f, big), axis=1, keepdims=True)
    el2 = jnp.where(lane_f == i1, -jnp.inf, el)
    v2 = jnp.max(el2, axis=1, keepdims=True)
    i2 = jnp.min(jnp.where(el2 == v2, lane_f, big), axis=1, keepdims=True)
    e21 = jnp.exp(v2 - v1)
    w1 = g_w / (1.0 + e21)
    return jnp.where(lane_f == i1, w1, 0.0) + jnp.where(lane_f == i2, w1 * e21, 0.0)


def _moe_kernel(ma_ref, mb_ref, x_ref, m2_ref, m3_ref, m4_ref, m5_ref, g_ref, woa_ref, wob_ref,
                wrh_ref, wrl_ref, wg_ref, wu_ref, wd_ref, gf_ref, o_ref,
                x1_ref, h_ref, comb_ref, acc_ref, *, final_norm):
    bb, tt, d = x_ref.shape
    tm = bb * tt
    e = pl.program_id(2)

    @pl.when(e == 0)
    def _():
        mix = _dot(ma_ref[...], woa_ref[...]) + _dot(mb_ref[...], wob_ref[...])
        x1 = x_ref[...] + m2_ref[...] * mix.reshape(bb, tt, d)
        h = (_rms(x1, g_ref[...]) * (1.0 + m4_ref[...]) + m3_ref[...]).reshape(tm, d)
        x1_ref[...] = x1.reshape(tm, d)
        h_hi = h.astype(BF16)
        h_lo = (h - h_hi.astype(F32)).astype(BF16)
        h_ref[...] = h_hi
        logits = _dot(h_hi, wrh_ref[...]) + (_dot(h_hi, wrl_ref[...]) + _dot(h_lo, wrh_ref[...]))
        comb_ref[...] = _route(logits)
        acc_ref[...] = jnp.zeros(acc_ref.shape, F32)

    hb = h_ref[...]
    a = _dot(hb, wg_ref[0])
    u = _dot(hb, wu_ref[0])
    lane = lax.broadcasted_iota(I32, (tm, LANES), 1)
    ce = jnp.sum(jnp.where(lane == e + N_GROUPS, comb_ref[...], 0.0), axis=1, keepdims=True)
    act = (a * _sigmoid(a)) * u * ce
    acc_ref[...] += _dot(act.astype(BF16), wd_ref[0])

    @pl.when(e == pl.num_programs(2) - 1)
    def _():
        out = x1_ref[...].reshape(bb, tt, d) + m5_ref[...] * acc_ref[...].reshape(bb, tt, d)
        if final_norm:
            out = _rms(out, gf_ref[...])
        o_ref[...] = out


def _moe(mix_a, mix_b, x, mod, g_ffn, woa, wob, wrh, wrl, wg, wu, wd, g_final, final_norm, bb, tt):
    bx, tx, d = x.shape
    tm = bb * tt
    blk = lambda n: pl.BlockSpec((bb, tt, n), lambda i, j, e: (i, j, 0))
    nj = tx // tt
    flat = lambda a: pl.BlockSpec((tm, a.shape[-1]), lambda i, j, e: (i * nj + j, 0))
    modspec = pl.BlockSpec((bb, 1, d), lambda i, j, e: (i, 0, 0))
    vec = pl.BlockSpec((1, 1, d), lambda i, j, e: (0, 0, 0))
    full = lambda a: pl.BlockSpec(a.shape, lambda i, j, e: (0,) * a.ndim)
    n_e, _, f = wg.shape
    return pl.pallas_call(
        functools.partial(_moe_kernel, final_norm=final_norm),
        grid=(bx // bb, tx // tt, n_e),
        in_specs=[flat(mix_a), flat(mix_b), blk(d), modspec, modspec, modspec, modspec, vec,
                  full(woa), full(wob), full(wrh), full(wrl),
                  pl.BlockSpec((1, d, f), lambda i, j, e: (e, 0, 0)),
                  pl.BlockSpec((1, d, f), lambda i, j, e: (e, 0, 0)),
                  pl.BlockSpec((1, f, d), lambda i, j, e: (e, 0, 0)), vec],
        out_specs=blk(d),
        out_shape=jax.ShapeDtypeStruct((bx, tx, d), F32),
        scratch_shapes=[pltpu.VMEM((tm, d), F32), pltpu.VMEM((tm, d), BF16),
                        pltpu.VMEM((tm, LANES), F32), pltpu.VMEM((tm, d), F32)],
        compiler_params=_params("arbitrary", "arbitrary", "arbitrary"), name="moe",
    )(mix_a.reshape(bx * tx, -1), mix_b.reshape(bx * tx, -1), x, mod[2], mod[3], mod[4], mod[5], g_ffn.reshape(1, 1, d),
      woa, wob, wrh, wrl, wg, wu, wd, g_final.reshape(1, 1, d))


def _head_major(x, nq, tq):
    b, _, h, f = x.shape
    return x.reshape(b, nq, tq, h, f).transpose(0, 1, 3, 2, 4).reshape(b, nq, h * tq, f)


def _token_major(x, tq, h):
    b, nq, _, f = x.shape
    return x.reshape(b, nq, h, tq, f).transpose(0, 1, 3, 2, 4).reshape(b, nq * tq, h, f)


def _lane_pad(x, slot, n_slots=2):
    parts = [jnp.where(jnp.asarray(np.asarray(slot) == s)[:, None], x, 0) for s in range(n_slots)]
    return jnp.concatenate(parts, axis=-1)


def _even_mixer(proj, kv_rows, mla_w, tabs, tab_blocks, tq, past, s_total):
    b, t, _ = proj.shape
    nq = t // tq
    ki, akv, kmla = kv_rows
    qi = _head_major(proj[..., EV_QI:EV_QI + 512].astype(BF16).reshape(b, t, IDX_HEADS, IDX_DH), nq, tq)
    wi = proj[..., EV_WI:EV_WI + IDX_HEADS] * (IDX_HEADS ** -0.5 * IDX_DH ** -0.5)
    qa = (proj[..., EV_QA:EV_QA + 512] * A_DH ** -0.5).astype(BF16).reshape(b, t, A_HEADS, A_DH)
    qa = _head_major(_lane_pad(qa, np.arange(A_HEADS) // (A_HEADS // A_KVH)), nq, tq)
    oa = _dsa(qi, wi, qa, ki, akv, tq, past, s_total)
    oa = _token_major(oa, tq, A_HEADS).reshape(b, t, A_HEADS, A_KVH, A_DH)
    hsel = jnp.asarray(np.arange(A_HEADS) // (A_HEADS // A_KVH))
    oa = jnp.take_along_axis(oa, hsel[None, None, :, None, None], axis=3).reshape(b, t, A_HEADS * A_DH)

    q_norm, wn, wr, wuk, wuv = mla_w
    qp = _mlaq(proj[..., EV_QLAT:EV_QLAT + B_QLORA].reshape(b * t, B_QLORA), q_norm, wn, wr, wuk, tabs, tab_blocks)
    if tq == QBLK:
        qp = qp.reshape(b, nq, B_HEADS * tq, 2 * LANES)
    else:
        per = QBLK // tq
        qp = qp.reshape(b // per, B_HEADS, per, tq, 2 * LANES).transpose(0, 2, 1, 3, 4).reshape(b, 1, B_HEADS * tq, 2 * LANES)
    ol = _mla(qp, kmla, tq, past)
    if tq == QBLK:
        ol = ol.reshape(b * nq, B_HEADS, tq, LANES)
    else:
        per = QBLK // tq
        ol = ol.reshape(b // per, per, B_HEADS, tq, LANES).transpose(0, 2, 1, 3, 4).reshape(b // per, B_HEADS, QBLK, LANES)
    ob = _mlao(ol, wuv).reshape(b, t, B_HEADS * B_VDH)
    return oa, ob


def _odd_mixer(proj, kv_rows, cmp_w, diff_w, tq, past, win_base, s_total, lam_init):
    b, t, _ = proj.shape
    nq = t // tq
    cmp_rows, kvsel, kvwin, dkv = kv_rows
    pe_rows, phi_big = cmp_w
    n_cmp = s_total // CMP_BLK
    kvc = _compress(cmp_rows, pe_rows, phi_big, n_cmp)
    nb = -(-n_cmp // (2 * LANES)) * 2 * LANES
    kvc = jnp.pad(kvc.astype(BF16), ((0, 0), (0, nb - n_cmp), (0, 0)))
    kvc = jnp.concatenate([kvc[:, 0::2], kvc[:, 1::2]], axis=1)
    zeros_slot = np.zeros(C_HEADS, np.int64)
    qc = (proj[..., OD_QC:OD_QC + 512] * C_DH ** -0.5).astype(BF16).reshape(b, t, C_HEADS, C_DH)
    qr = (proj[..., OD_QCR:OD_QCR + 512] * C_DH ** -0.5).astype(BF16).reshape(b, t, C_HEADS, C_DH)
    qc = _head_major(_lane_pad(qc, zeros_slot), nq, tq)
    qr = _head_major(_lane_pad(qr, zeros_slot), nq, tq)
    gates = proj[..., OD_GATE:OD_GATE + LANES]
    oc = _nsa(qc, qr, gates, kvc, kvsel, kvwin, tq, past, win_base, s_total)
    oc = _token_major(oc, tq, C_HEADS)[..., C_DH:].reshape(b, t, C_HEADS * C_DH)

    lam_vec, subln = diff_w
    qd = (proj[..., OD_QD:OD_QD + 512] * D_DH ** -0.5).astype(BF16)
    rep = D_HEADS // D_KVH
    qd = qd.reshape(b, t, D_KVH, rep, 2, D_DH).transpose(0, 1, 2, 4, 3, 5)
    qd = qd.reshape(b, t, D_KVH * 2 * rep, D_DH)
    slot = np.tile(np.repeat(np.arange(2), rep), D_KVH)
    qd = _head_major(_lane_pad(qd, slot), nq, tq)
    od = _diff(qd, dkv, lam_vec, subln, tq, past, lam_init)
    return oc, od.astype(BF16)


def kernel(x_prompt, x_sample, c_prompt, c_sample, cache_a_kv, cache_a_idx, cache_b_latent, cache_b_krope,
           cache_c_cmp, cache_c_sel, state_c_win, cache_d_kv, page_table, w_ada, b_ada, g_mix, g_ffn, w_out,
           w_in_even, b_q_norm, b_kv_norm, w_b_qb, w_b_kvb, w_in_odd, c_phi_k, c_phi_v, c_pe_k, c_pe_v,
           d_lambda, d_subln, moe_w_group, moe_w_expert, moe_w_gate, moe_w_up, moe_w_down, g_final):
    depth = w_ada.shape[0]
    bp, tp, d = x_prompt.shape
    bs, ts, _ = x_sample.shape
    n_pages = page_table.shape[1]
    past = n_pages * PAGE_SIZE
    win_buf = state_c_win.shape[2]
    n_pool = cache_a_kv.shape[1]
    assert tp % KEY_CHUNK == 0 and QBLK % ts == 0 and (bs * ts) % QBLK == 0

    bc = -(-(bp + bs) // 8) * 8
    c_all = jnp.pad(jnp.concatenate([c_prompt, c_sample]), ((0, bc - bp - bs), (0, 0)))
    mod_all = _ada(c_all, w_ada, b_ada).reshape(depth, bc, 6, 1, d)

    pos_p = jnp.arange(tp, dtype=I32)
    pos_s = past + jnp.arange(ts, dtype=I32)
    tabs_p = _all_tables(pos_p)
    bb_s = min(bs, 64)
    tabs_s = jnp.tile(_all_tables(pos_s), (1, max(bb_s, QBLK // ts), 1))

    xp, xs = x_prompt, x_sample
    rows = {k: [] for k in ("akv_p", "akv_s", "idx_p", "idx_s", "lat_p", "lat_s", "kr_p", "kr_s",
                            "cmp_p", "cmp_s", "sel_p", "sel_s", "win_p", "win_s", "dkv_p", "dkv_s")}
    tm_p = 512
    for l in range(depth):
        li = l // 2
        mod_p = [mod_all[l, :bp, k] for k in range(6)]
        mod_s = [mod_all[l, bp:bp + bs, k] for k in range(6)]
        if l % 2 == 0:
            w, groups = _even_weights(w_in_even[li])
            kvn = b_kv_norm[li].reshape(1, LANES)
            pp = _proj(xp, mod_p[0], mod_p[1], g_mix[l], w, tabs_p, kvn, groups, EV_OUT, 1, tm_p)
            ps = _proj(xs, mod_s[0], mod_s[1], g_mix[l], w, tabs_s, kvn, groups, EV_OUT, bb_s, ts)
            wqb = w_b_qb[li]
            pad_h = lambda a: jnp.pad(a, ((0, 0), (0, 0), (0, LANES - a.shape[2]))).reshape(a.shape[0], -1)
            wn = pad_h(wqb[..., :B_NOPE]).astype(BF16)
            wr = pad_h(wqb[..., B_NOPE:]).astype(BF16)
            wkvb = w_b_kvb[li]
            wuk = jnp.pad(wkvb[..., :B_NOPE].transpose(1, 2, 0), ((0, 0), (0, LANES - B_NOPE), (0, 0))).astype(BF16)
            wuv = wkvb[..., B_NOPE:].transpose(1, 0, 2)
            wuv = _lane_pad(wuv.transpose(1, 0, 2), np.arange(B_HEADS) % 2).transpose(1, 0, 2).astype(BF16)
            mla_w = (b_q_norm[li].reshape(1, B_QLORA), wn, wr, wuk, wuv)

            def kv_prompt(p):
                kmla = jnp.concatenate([p[..., EV_CKV:EV_CKV + LANES], p[..., EV_KR:EV_KR + B_ROPE],
                                        jnp.zeros(p.shape[:2] + (LANES - B_ROPE,), F32)], axis=-1)
                return (p[..., EV_KI:EV_KI + IDX_DH].astype(BF16), p[..., EV_AKV:EV_AKV + 256].astype(BF16),
                        kmla.astype(BF16))

            oa_p, ob_p = _even_mixer(pp, kv_prompt(pp), mla_w, tabs_p, tp // QBLK, QBLK, 0, tp)
            new_s = (ps[..., EV_KI:EV_KI + IDX_DH], ps[..., EV_AKV:EV_AKV + 256],
                     ps[..., EV_CKV:EV_CKV + LANES], ps[..., EV_KR:EV_KR + B_ROPE])
            ki_s = _gather(page_table, [cache_a_idx], li, [new_s[0]], IDX_DH)
            akv_s = _gather(page_table, [cache_a_kv.reshape(cache_a_kv.shape[:3] + (256,))], li, [new_s[1]], 256)
            kmla_s = _gather(page_table, [cache_b_latent, cache_b_krope], li, [new_s[2], new_s[3]], 2 * LANES)
            oa_s, ob_s = _even_mixer(ps, (ki_s, akv_s, kmla_s), mla_w, tabs_s, 1, ts, past, past + ts)
            for tag, p in (("p", pp), ("s", ps)):
                rows["akv_" + tag].append(p[..., EV_AKV:EV_AKV + 256].reshape(p.shape[:2] + (2, A_KVH, A_DH)))
                rows["idx_" + tag].append(p[..., EV_KI:EV_KI + IDX_DH])
                rows["lat_" + tag].append(p[..., EV_CKV:EV_CKV + B_KVLORA])
                rows["kr_" + tag].append(p[..., EV_KR:EV_KR + B_ROPE])
            mix_p, mix_s = (oa_p, ob_p), (oa_s, ob_s)
        else:
            w, groups = _odd_weights(w_in_odd[li])
            kvn = jnp.ones((1, LANES), F32)
            pp = _proj(xp, mod_p[0], mod_p[1], g_mix[l], w, tabs_p, kvn, groups, OD_OUT, 1, tm_p)
            ps = _proj(xs, mod_s[0], mod_s[1], g_mix[l], w, tabs_s, kvn, groups, OD_OUT, bb_s, ts)
            lam_init = 0.8 - 0.6 * math.exp(-0.3 * l)
            zk = jnp.zeros_like(c_phi_k[li])
            phi_big = jnp.concatenate([jnp.stack([c_phi_k[li], zk], axis=1), jnp.stack([zk, c_phi_v[li]], axis=1)],
                                      axis=-1).reshape(CMP_BLK * 2 * C_DH, 2 * C_DH).astype(BF16)
            pe = jnp.stack([c_pe_k[li], c_pe_v[li]], axis=1).reshape(1, CMP_BLK * 2 * C_DH)
            cmp_w = (jnp.tile(pe, (8, 1)).astype(BF16), phi_big)
            diff_w = (d_lambda[li], d_subln[li].reshape(1, 2 * D_DH))
            blk_rows = lambda a: a.reshape(a.shape[0], a.shape[1] // CMP_BLK, CMP_BLK * 2 * C_DH)
            kv_p = (blk_rows(pp[..., OD_CMP:OD_CMP + LANES].astype(BF16)),
                    pp[..., OD_SEL:OD_SEL + LANES].astype(BF16), pp[..., OD_WIN:OD_WIN + LANES].astype(BF16),
                    pp[..., OD_DKV:OD_DKV + 512].astype(BF16))
            oc_p, od_p = _odd_mixer(pp, kv_p, cmp_w, diff_w, QBLK, 0, 0, tp, lam_init)
            flat = lambda a: a.reshape(a.shape[:3] + (-1,))
            new_cmp, new_sel = ps[..., OD_CMP:OD_CMP + LANES], ps[..., OD_SEL:OD_SEL + LANES]
            new_win, new_dkv = ps[..., OD_WIN:OD_WIN + LANES], ps[..., OD_DKV:OD_DKV + 512]
            cmp_s = _gather(page_table, [flat(cache_c_cmp)], li, [new_cmp], LANES)
            sel_s = _gather(page_table, [flat(cache_c_sel)], li, [new_sel], LANES)
            dkv_s = _gather(page_table, [flat(cache_d_kv)], li, [new_dkv], 4 * LANES)
            win_full = jnp.concatenate([state_c_win[li].reshape(bs, win_buf, LANES), new_win], axis=1)
            w_rows = -(-(win_buf + ts) // LANES) * LANES
            win_s = jnp.pad(win_full, ((0, 0), (0, w_rows - win_buf - ts), (0, 0))).astype(BF16)
            oc_s, od_s = _odd_mixer(ps, (blk_rows(cmp_s), sel_s, win_s, dkv_s), cmp_w, diff_w,
                                    ts, past, past - win_buf, past + ts, lam_init)
            keep_p = min(WINDOW, tp)
            rows["win_p"].append(pp[:, tp - keep_p:, OD_WIN:OD_WIN + LANES].reshape(bp, keep_p, 2, C_DH))
            rows["win_s"].append(win_full[:, ts:].reshape(bs, win_buf, 2, C_DH))
            for tag, p in (("p", pp), ("s", ps)):
                rows["cmp_" + tag].append(p[..., OD_CMP:OD_CMP + LANES].reshape(p.shape[:2] + (2, C_DH)))
                rows["sel_" + tag].append(p[..., OD_SEL:OD_SEL + LANES].reshape(p.shape[:2] + (2, C_DH)))
                rows["dkv_" + tag].append(p[..., OD_DKV:OD_DKV + 512].reshape(p.shape[:2] + (2, D_KVH, 2 * D_DH)))
            mix_p, mix_s = (oc_p, od_p), (oc_s, od_s)

        woa, wob = w_out[l, :512].astype(BF16), w_out[l, 512:].astype(BF16)
        wr_full = jnp.concatenate([moe_w_group[l], moe_w_expert[l],
                                   jnp.zeros((d, LANES - N_GROUPS - N_EXPERTS), F32)], axis=1)
        wrh = wr_full.astype(BF16)
        wrl = (wr_full - wrh.astype(F32)).astype(BF16)
        wg, wu, wd = moe_w_gate[l].astype(BF16), moe_w_up[l].astype(BF16), moe_w_down[l].astype(BF16)
        last = l == depth - 1
        xp = _moe(mix_p[0], mix_p[1], xp, mod_p, g_ffn[l], woa, wob, wrh, wrl, wg, wu, wd, g_final, last,
                  1, min(tp, 1024))
        xs = _moe(mix_s[0], mix_s[1], xs, mod_s, g_ffn[l], woa, wob, wrh, wrl, wg, wu, wd, g_final, last,
                  bs, ts)

    st = lambda k: jnp.stack(rows[k])
    return (xp, xs, st("akv_p"), st("akv_s"), st("idx_p"), st("idx_s"), st("lat_p"), st("lat_s"),
            st("kr_p"), st("kr_s"), st("cmp_p"), st("cmp_s"), st("sel_p"), st("sel_s"),
            st("win_p"), st("win_s"), st("dkv_p"), st("dkv_s"))
```

```python
import functools
import math

import jax
import jax.numpy as jnp
import numpy as np
from jax import lax
from jax.experimental import pallas as pl
from jax.experimental.pallas import tpu as pltpu

F32 = jnp.float32
BF16 = jnp.bfloat16
I32 = jnp.int32

D_MODEL = 1024
PAGE_SIZE = 128
QBLK = 128
ROPE_THETA = 10000.0
EPS = 1e-6
NEG = -1e30
FORCE_SCORE = 1e9

A_HEADS, A_DH, A_KVH = 8, 64, 2
IDX_HEADS, IDX_DH, IDX_TOPK_MAX = 8, 64, 256
B_HEADS, B_NOPE, B_ROPE, B_VDH, B_QLORA, B_KVLORA = 8, 64, 32, 64, 256, 128
C_HEADS, C_DH, CMP_BLK, SEL_BLK, N_SEL, WINDOW = 8, 64, 32, 64, 16, 512
D_HEADS, D_DH, D_KVH = 4, 64, 2
N_GROUPS, E_PER_GROUP, E_HIDDEN = 4, 4, 256
N_EXPERTS = N_GROUPS * E_PER_GROUP

LANES = 128
KEY_CHUNK = 512
GATHER_PAGES = 8
VMEM_LIMIT = 56 * 1024 * 1024

EV_QA, EV_QI, EV_AKV, EV_KI, EV_CKV, EV_QLAT, EV_KR = 0, 512, 1024, 1280, 1408, 1536, 1792
EV_WI = EV_KR + B_ROPE
EV_OUT = 1920
OD_QC, OD_QCR, OD_QD, OD_DKV, OD_CMP, OD_SEL, OD_WIN, OD_GATE = 0, 512, 1024, 1536, 2048, 2176, 2304, 2432
OD_OUT = 2560


def _dot(a, b):
    return jnp.dot(a, b, preferred_element_type=F32)


def _dot_nt(a, b):
    return lax.dot_general(a, b, (((1,), (1,)), ((), ())), preferred_element_type=F32)


def _params(*sem):
    return pltpu.CompilerParams(dimension_semantics=sem, vmem_limit_bytes=VMEM_LIMIT)


def _rms(x, g):
    return x * lax.rsqrt(jnp.mean(x * x, axis=-1, keepdims=True) + EPS) * g


def _sigmoid(x):
    return 1.0 / (1.0 + jnp.exp(-x))


def _ada_kernel(c_ref, w_ref, b_ref, o_ref):
    c = c_ref[...]
    s = (c * _sigmoid(c)).astype(BF16)
    o_ref[0] = _dot(s, w_ref[0].astype(BF16)) + b_ref[0]


def _ada(c_all, w_ada, b_ada):
    depth, d, n = w_ada.shape
    bc = c_all.shape[0]
    nb = 1536
    return pl.pallas_call(
        _ada_kernel,
        grid=(depth, n // nb),
        in_specs=[pl.BlockSpec((bc, d), lambda l, j: (0, 0)),
                  pl.BlockSpec((1, d, nb), lambda l, j: (l, 0, j)),
                  pl.BlockSpec((1, 1, nb), lambda l, j: (l, 0, j))],
        out_specs=pl.BlockSpec((1, bc, nb), lambda l, j: (l, 0, j)),
        out_shape=jax.ShapeDtypeStruct((depth, bc, n), F32),
        compiler_params=_params("arbitrary", "arbitrary"), name="ada",
    )(c_all, w_ada, b_ada.reshape(depth, 1, n))


def _rope_tables(pos, width, active):
    half = width // 2
    inv = ROPE_THETA ** (-jnp.arange(half, dtype=F32) / half)
    ang = pos.astype(F32)[:, None] * inv[None, :]
    cos, sin = jnp.cos(ang), jnp.sin(ang)
    lane = np.arange(LANES)
    j = lane % width
    cos_l = cos[:, j % half]
    sin_l = sin[:, j % half] * jnp.asarray(np.where(j < half, -1.0, 1.0), F32)[None, :]
    act = jnp.asarray(lane < active)[None, :]
    return jnp.where(act, cos_l, 1.0), jnp.where(act, sin_l, 0.0)


def _all_tables(pos):
    t = []
    for width, active in ((64, 128), (64, 64), (32, 32)):
        t.extend(_rope_tables(pos, width, active))
    return jnp.stack(t)


TAB = {"rope64": 0, "rope64h": 2, "rope32": 4}
WIDTH = {"rope64": 64, "rope64h": 64, "rope32": 32}


def _rope(y, cos, sin, width):
    half = width // 2
    lane = lax.broadcasted_iota(I32, y.shape, 1)
    first = (lane & (width - 1)) < half
    up = pltpu.roll(y, LANES - half, axis=1)
    dn = pltpu.roll(y, half, axis=1)
    return y * cos + jnp.where(first, up, dn) * sin


def _proj_kernel(x_ref, sh_ref, sc_ref, g_ref, w_ref, tab_ref, kvn_ref, o_ref, *, chunks, groups):
    bb, tt, d = x_ref.shape
    h = _rms(x_ref[...], g_ref[...]) * (1.0 + sc_ref[...]) + sh_ref[...]
    hb = h.reshape(bb * tt, d).astype(BF16)
    for c0, cw in chunks:
        y = _dot(hb, w_ref[:, c0:c0 + cw])
        for k in range(cw // LANES):
            yg = y[:, k * LANES:(k + 1) * LANES]
            for kind, oc in groups[c0 // LANES + k]:
                if kind == "plain":
                    v = yg
                elif kind == "rms":
                    v = _rms(yg, kvn_ref[...])
                elif kind == "sigmoid":
                    v = _sigmoid(yg)
                else:
                    v = _rope(yg, tab_ref[TAB[kind]], tab_ref[TAB[kind] + 1], WIDTH[kind])
                o_ref[:, :, oc:oc + LANES] = v.reshape(bb, tt, LANES)


def _proj(x, shift, scale, g, w, tabs, kvn, groups, n_out, bb, tt):
    bx, tx, d = x.shape
    nw = w.shape[1]
    chunks = [(c0, min(256, nw - c0)) for c0 in range(0, nw, 256)]
    rows = bb * tt
    tab_map = (lambda i, j: (0, j, 0)) if bb == 1 else (lambda i, j: (0, 0, 0))
    return pl.pallas_call(
        functools.partial(_proj_kernel, chunks=chunks, groups=groups),
        grid=(bx // bb, tx // tt),
        in_specs=[pl.BlockSpec((bb, tt, d), lambda i, j: (i, j, 0)),
                  pl.BlockSpec((bb, 1, d), lambda i, j: (i, 0, 0)),
                  pl.BlockSpec((bb, 1, d), lambda i, j: (i, 0, 0)),
                  pl.BlockSpec((1, 1, d), lambda i, j: (0, 0, 0)),
                  pl.BlockSpec((d, nw), lambda i, j: (0, 0)),
                  pl.BlockSpec((6, rows, LANES), tab_map),
                  pl.BlockSpec((1, LANES), lambda i, j: (0, 0))],
        out_specs=pl.BlockSpec((bb, tt, n_out), lambda i, j: (i, j, 0)),
        out_shape=jax.ShapeDtypeStruct((bx, tx, n_out), F32),
        compiler_params=_params("arbitrary", "arbitrary"), name="proj",
    )(x, shift, scale, g.reshape(1, 1, d), w, tabs, kvn)


def _even_weights(w_in):
    d = w_in.shape[0]
    qa, ka, va, qi, wi, ki, qlat, kvlat = jnp.split(
        w_in, [512, 640, 768, 1280, 1288, 1352, 1608], axis=1)
    z = lambda n: jnp.zeros((d, n), w_in.dtype)
    w = jnp.concatenate([qa, qi, ka, va, ki, z(64), kvlat[:, :B_KVLORA], qlat,
                         kvlat[:, B_KVLORA:], wi, z(LANES - B_ROPE - IDX_HEADS)], axis=1)
    groups = ([[("rope64", EV_QA + LANES * k)] for k in range(4)]
              + [[("rope64", EV_QI + LANES * k)] for k in range(4)]
              + [[("rope64", EV_AKV)], [("plain", EV_AKV + LANES)], [("rope64h", EV_KI)],
                 [("rms", EV_CKV)], [("plain", EV_QLAT)], [("plain", EV_QLAT + LANES)],
                 [("rope32", EV_KR)]])
    return w.astype(BF16), groups


def _odd_weights(w_in):
    d = w_in.shape[0]
    qc, kvc, kvs, kvw, gc, qd, kd, vd = jnp.split(
        w_in, [512, 640, 768, 896, 920, 1432, 1688], axis=1)
    w = jnp.concatenate([qc, qd, kd, vd, kvc, kvs, kvw, gc,
                         jnp.zeros((d, LANES - 3 * C_HEADS), w_in.dtype)], axis=1)
    groups = ([[("plain", OD_QC + LANES * k), ("rope64", OD_QCR + LANES * k)] for k in range(4)]
              + [[("rope64", OD_QD + LANES * k)] for k in range(4)]
              + [[("rope64", OD_DKV)], [("rope64", OD_DKV + LANES)],
                 [("plain", OD_DKV + 2 * LANES)], [("plain", OD_DKV + 3 * LANES)],
                 [("plain", OD_CMP)], [("rope64h", OD_SEL)], [("rope64h", OD_WIN)],
                 [("sigmoid", OD_GATE)]])
    return w.astype(BF16), groups


def _gather_kernel(pt_ref, *refs, n_pools, widths, offs, modes):
    g = GATHER_PAGES
    pages = refs[:n_pools * g]
    news = refs[n_pools * g:n_pools * g + n_pools]
    o_ref = refs[-1]
    j = pl.program_id(1)
    last = pl.num_programs(1) - 1
    o_ref[...] = jnp.zeros(o_ref.shape, o_ref.dtype)

    @pl.when(j < last)
    def _():
        eye = jnp.where(lax.broadcasted_iota(I32, (PAGE_SIZE, PAGE_SIZE), 0)
                        == lax.broadcasted_iota(I32, (PAGE_SIZE, PAGE_SIZE), 1), 1.0, 0.0).astype(BF16)
        for p in range(n_pools):
            for k in range(g):
                page = pages[p * g + k]
                r = slice(k * PAGE_SIZE, (k + 1) * PAGE_SIZE)
                if modes[p] == "rows":
                    o_ref[0, r, offs[p]:offs[p] + widths[p]] = page[0, 0].astype(BF16)
                elif modes[p] == "cols":
                    o_ref[0, r, offs[p]:offs[p] + widths[p]] = _dot_nt(eye, page[0, 0].astype(BF16)).astype(BF16)
                else:
                    for q in range(4):
                        o_ref[0, r, offs[p] + q * LANES:offs[p] + (q + 1) * LANES] = (
                            page[0, 0, pl.ds(q, PAGE_SIZE, stride=4), :].astype(BF16))

    @pl.when(j == last)
    def _():
        for p in range(n_pools):
            tn = news[p].shape[1]
            o_ref[0, 0:tn, offs[p]:offs[p] + widths[p]] = news[p][0].astype(BF16)


def _page_view(pool):
    l, n, rows = pool.shape[:3]
    feat = int(np.prod(pool.shape[3:]))
    if pool.shape[-1] >= LANES and pool.ndim == 4:
        return pool, "rows", feat
    if pool.shape[-1] >= LANES:
        assert feat == 4 * LANES
        return pool.reshape(l, n, rows * 4, LANES), "quad", feat
    perm = (0, 1) + tuple(range(3, pool.ndim)) + (2,)
    return pool.transpose(perm).reshape(l, n, feat, rows), "cols", feat


def _gather(page_table, pools, li, news, f_tot):
    g = GATHER_PAGES
    b, n_pages = page_table.shape
    assert n_pages % g == 0
    nsteps = n_pages // g + 1
    views = [_page_view(p) for p in pools]
    pools = [v[0] for v in views]
    modes = [v[1] for v in views]
    widths = [v[2] for v in views]
    offs = list(np.cumsum([0] + widths[:-1]))
    in_specs = []
    args = []
    for p in pools:
        for k in range(g):
            in_specs.append(pl.BlockSpec(
                (1, 1) + p.shape[2:],
                lambda i, j, pt, k=k: (li, pt[i, jnp.minimum(j, nsteps - 2) * g + k], 0, 0)))
            args.append(p)
    for nw in news:
        in_specs.append(pl.BlockSpec((1,) + nw.shape[1:], lambda i, j, pt: (i, 0, 0)))
        args.append(nw)
    rows = g * PAGE_SIZE
    return pl.pallas_call(
        functools.partial(_gather_kernel, n_pools=len(pools), widths=widths, offs=offs, modes=modes),
        grid_spec=pltpu.PrefetchScalarGridSpec(
            num_scalar_prefetch=1, grid=(b, nsteps), in_specs=in_specs,
            out_specs=pl.BlockSpec((1, rows, f_tot), lambda i, j, pt: (i, j, 0))),
        out_shape=jax.ShapeDtypeStruct((b, nsteps * rows, f_tot), BF16),
        compiler_params=_params("arbitrary", "arbitrary"), name="gather",
    )(page_table, *args)


def _online_update(s, mask, v, state):
    m_ref, l_ref, acc_ref = state
    if mask is not None:
        s = jnp.where(mask, s, 2 * NEG)
    m_old = m_ref[...]
    m_new = jnp.maximum(m_old, jnp.max(s, axis=1, keepdims=True))
    p = jnp.exp(s - pltpu.repeat(m_new, s.shape[1] // LANES, axis=1))
    alpha = jnp.exp(m_old - m_new)
    l_ref[...] = alpha * l_ref[...] + jnp.sum(p, axis=1, keepdims=True)
    acc_ref[...] = alpha * acc_ref[...] + _dot(p.astype(BF16), v)
    m_ref[...] = m_new


def _init_state(state):
    m_ref, l_ref, acc_ref = state
    m_ref[...] = jnp.full(m_ref.shape, NEG, F32)
    l_ref[...] = jnp.zeros(l_ref.shape, F32)
    acc_ref[...] = jnp.zeros(acc_ref.shape, F32)


def _finish(state):
    m_ref, l_ref, acc_ref = state
    return acc_ref[...] / jnp.maximum(l_ref[...], 1e-30)


def _state_scratch(rows, n):
    return [pltpu.VMEM((rows, LANES), F32) for _ in range(3 * n)]


def _attend_once(s, mask, v):
    if mask is not None:
        s = jnp.where(mask, s, NEG)
    e = jnp.exp(s - jnp.max(s, axis=1, keepdims=True))
    if mask is not None:
        e = jnp.where(mask, e, 0.0)
    return _dot(e.astype(BF16), v) / jnp.maximum(jnp.sum(e, axis=1, keepdims=True), 1e-30)


def _round_up(x, m):
    return -(-x // m) * m


ONCE_MAX_ROWS = 64


def _once_len(nq, rows, past, tq):
    return _round_up(past + tq, KEY_CHUNK) if nq == 1 and rows <= ONCE_MAX_ROWS else 0


def _row_pos(rows, tq, q0):
    r = lax.broadcasted_iota(I32, (rows, 1), 0)
    return q0 + (r & (tq - 1))


def _sort_key(x):
    b = pltpu.bitcast(x + 0.0, I32)
    return b ^ ((b >> 31) & 0x7FFFFFFF)


def _count(load, nchunks, width, rows, pred):
    lane = lax.broadcasted_iota(I32, (rows, LANES), 1)

    def body(c, acc):
        k = load(c)
        for j in range(width // LANES):
            idx = c * width + j * LANES + lane
            acc = acc + jnp.where(pred(k[:, j * LANES:(j + 1) * LANES], idx), 1.0, 0.0)
        return acc

    zero = jnp.zeros((rows, LANES), F32)
    acc = body(0, zero) if isinstance(nchunks, int) and nchunks == 1 else lax.fori_loop(0, nchunks, body, zero)
    return jnp.sum(acc, axis=1, keepdims=True)


def _topk_threshold(load, nchunks, width, rows, k, idx_bits, j_ref):
    kf = float(k)
    int_min = jnp.int32(-2 ** 31)

    def bit_step(it, t):
        cand = t ^ lax.shift_left(jnp.int32(1), jnp.int32(31) - it)
        cnt = _count(load, nchunks, width, rows, lambda key, idx: key >= cand)
        return jnp.where(cnt >= kf, cand, t)

    t = lax.fori_loop(0, 32, bit_step, jnp.full((rows, LANES), int_min, I32))
    cnt_ge = _count(load, nchunks, width, rows, lambda key, idx: key >= t)
    j_ref[...] = jnp.full((rows, LANES), 2 ** 30, I32)

    @pl.when(jnp.max(cnt_ge) > kf)
    def _():
        need = kf - _count(load, nchunks, width, rows, lambda key, idx: key > t)

        def idx_step(it, p):
            cand = p | lax.shift_left(jnp.int32(1), jnp.int32(idx_bits - 1) - it)
            cnt = _count(load, nchunks, width, rows, lambda key, idx: (key == t) & (idx < cand))
            return jnp.where(cnt < need, cand, p)

        j_ref[...] = lax.fori_loop(0, idx_bits, idx_step, jnp.zeros((rows, LANES), I32))

    return t, j_ref[...]


def _selected(key, idx, t, j):
    n = key.shape[1] // LANES
    if n > 1:
        t, j = pltpu.repeat(t, n, axis=1), pltpu.repeat(j, n, axis=1)
    return jnp.where(key > t, 1.0, jnp.where(key == t, jnp.where(idx <= j, 1.0, 0.0), 0.0))


def _tile_rows(x, n):
    return jnp.concatenate([x] * n, axis=0)


def _index_scores(s, wi, tq):
    acc = jnp.zeros((tq, s.shape[1]), F32)
    for h in range(IDX_HEADS):
        acc = acc + jnp.maximum(s[h * tq:(h + 1) * tq], 0.0) * wi[:, h:h + 1]
    return acc


def _dsa_kernel(qi_ref, wi_ref, qa_ref, ki_ref, kv_ref, o_ref, key_ref, j_ref, *scr,
                tq, past, k_top, idx_bits, n_once):
    ch = KEY_CHUNK
    q0 = past + pl.program_id(1) * tq
    qpos = q0 + lax.broadcasted_iota(I32, (tq, 1), 0)
    qi = qi_ref[0, 0]
    wi = wi_ref[0]
    rep = A_HEADS // A_KVH
    rr = rep * tq

    if n_once:
        lane = lax.broadcasted_iota(I32, (tq, n_once), 1)
        score = jnp.where(lane <= qpos, _index_scores(_dot_nt(qi, ki_ref[0, :n_once, :]), wi, tq), -jnp.inf)
        keys = _sort_key(score)
        t, j = _topk_threshold(lambda c: keys, 1, n_once, tq, k_top, idx_bits, j_ref)
        sel = jnp.where(lane <= qpos, _selected(keys, lane, t, j), 0.0)
        mask = _tile_rows(sel, rep) > 0.0
        kk, vv = kv_ref[0, :n_once, :LANES], kv_ref[0, :n_once, LANES:]
        for g in range(A_KVH):
            o = _attend_once(_dot_nt(qa_ref[0, 0, g * rr:(g + 1) * rr, :], kk), mask, vv)
            o_ref[0, 0, g * rr:(g + 1) * rr, :] = o.astype(BF16)
        return

    nc = (q0 + tq - 1) // ch + 1
    lane = lax.broadcasted_iota(I32, (tq, ch), 1)

    def score_step(c, carry):
        off = pl.multiple_of(c * ch, ch)
        acc = _index_scores(_dot_nt(qi, ki_ref[0, pl.ds(off, ch), :]), wi, tq)
        key_ref[:, pl.ds(off, ch)] = _sort_key(jnp.where(off + lane <= qpos, acc, -jnp.inf))
        return carry

    lax.fori_loop(0, nc, score_step, 0)
    load = lambda c: key_ref[:, pl.ds(pl.multiple_of(c * ch, ch), ch)]
    t, j = _topk_threshold(load, nc, ch, tq, k_top, idx_bits, j_ref)

    states = [scr[3 * g:3 * g + 3] for g in range(A_KVH)]
    for st in states:
        _init_state(st)

    def attend_step(c, carry):
        off = pl.multiple_of(c * ch, ch)
        idx = off + lane
        sel = jnp.where(idx <= qpos, _selected(key_ref[:, pl.ds(off, ch)], idx, t, j), 0.0)
        mask = _tile_rows(sel, rep) > 0.0
        kv = kv_ref[0, pl.ds(off, ch), :]
        kk, vv = kv[:, :LANES], kv[:, LANES:]
        for g in range(A_KVH):
            _online_update(_dot_nt(qa_ref[0, 0, g * rr:(g + 1) * rr, :], kk), mask, vv, states[g])
        return carry

    lax.fori_loop(0, nc, attend_step, 0)
    for g in range(A_KVH):
        o_ref[0, 0, g * rr:(g + 1) * rr, :] = _finish(states[g]).astype(BF16)


def _dsa(qi, wi, qa, ki, kv, tq, past, s_total):
    b, nq, rows, _ = qi.shape
    s_pad = ki.shape[1]
    k_top = min(IDX_TOPK_MAX, s_total // 4)
    rep_rows = rows // A_KVH
    n_once = _once_len(nq, rows, past, tq)
    return pl.pallas_call(
        functools.partial(_dsa_kernel, tq=tq, past=past, k_top=k_top,
                          idx_bits=int(math.ceil(math.log2(s_pad))), n_once=n_once),
        grid=(b, nq),
        in_specs=[pl.BlockSpec((1, 1, rows, IDX_DH), lambda i, j: (i, j, 0, 0)),
                  pl.BlockSpec((1, tq, IDX_HEADS), lambda i, j: (i, j, 0)),
                  pl.BlockSpec((1, 1, rows, LANES), lambda i, j: (i, j, 0, 0)),
                  pl.BlockSpec((1, s_pad, IDX_DH), lambda i, j: (i, 0, 0)),
                  pl.BlockSpec((1, s_pad, 2 * LANES), lambda i, j: (i, 0, 0))],
        out_specs=pl.BlockSpec((1, 1, rows, LANES), lambda i, j: (i, j, 0, 0)),
        out_shape=jax.ShapeDtypeStruct((b, nq, rows, LANES), BF16),
        scratch_shapes=[pltpu.VMEM((tq, LANES if n_once else s_pad), I32), pltpu.VMEM((tq, LANES), I32)]
        + _state_scratch(rep_rows, 0 if n_once else A_KVH),
        compiler_params=_params("arbitrary", "arbitrary"), name="dsa",
    )(qi, wi, qa, ki, kv)


def _mlaq_kernel(ql_ref, qn_ref, wn_ref, wr_ref, wuk_ref, tab_ref, o_ref):
    scale = (B_NOPE + B_ROPE) ** -0.5
    rb = _rms(ql_ref[...], qn_ref[...]).astype(BF16)
    qn_all = _dot(rb, wn_ref[...])
    qr_all = _dot(rb, wr_ref[...])
    for h in range(B_HEADS):
        sl = slice(h * LANES, (h + 1) * LANES)
        q_lat = _dot(qn_all[:, sl].astype(BF16), wuk_ref[h])
        q_rope = _rope(qr_all[:, sl], tab_ref[TAB["rope32"]], tab_ref[TAB["rope32"] + 1], B_ROPE)
        o_ref[0, h, :, 0:LANES] = (q_lat * scale).astype(BF16)
        o_ref[0, h, :, LANES:2 * LANES] = (q_rope * scale).astype(BF16)


def _mlaq(qlat, q_norm, wn, wr, wuk, tabs, tab_blocks):
    n = qlat.shape[0]
    nb = n // QBLK
    tab_map = lambda i: (0, i % tab_blocks, 0)
    return pl.pallas_call(
        _mlaq_kernel,
        grid=(nb,),
        in_specs=[pl.BlockSpec((QBLK, B_QLORA), lambda i: (i, 0)),
                  pl.BlockSpec((1, B_QLORA), lambda i: (0, 0)),
                  pl.BlockSpec((B_QLORA, B_HEADS * LANES), lambda i: (0, 0)),
                  pl.BlockSpec((B_QLORA, B_HEADS * LANES), lambda i: (0, 0)),
                  pl.BlockSpec((B_HEADS, LANES, LANES), lambda i: (0, 0, 0)),
                  pl.BlockSpec((6, QBLK, LANES), tab_map)],
        out_specs=pl.BlockSpec((1, B_HEADS, QBLK, 2 * LANES), lambda i: (i, 0, 0, 0)),
        out_shape=jax.ShapeDtypeStruct((nb, B_HEADS, QBLK, 2 * LANES), BF16),
        compiler_params=_params("arbitrary"), name="mlaq",
    )(qlat, q_norm, wn, wr, wuk, tabs)


def _causal_loops(step, q0, tq):
    ch = KEY_CHUNK
    nfull = q0 // ch
    lax.fori_loop(0, nfull, functools.partial(step, False), 0)
    lax.fori_loop(nfull, (q0 + tq - 1) // ch + 1, functools.partial(step, True), 0)


def _mla_kernel(q_ref, kv_ref, o_ref, *state, tq, past, n_once):
    ch = KEY_CHUNK
    rows = q_ref.shape[2]
    q0 = past + pl.program_id(1) * tq
    qpos = _row_pos(rows, tq, q0)
    q = q_ref[0, 0]

    if n_once:
        mask = lax.broadcasted_iota(I32, (1, n_once), 1) <= qpos
        o = _attend_once(_dot_nt(q, kv_ref[0, :n_once, :]), mask, kv_ref[0, :n_once, :LANES])
        o_ref[0, 0] = o.astype(BF16)
        return

    lane = lax.broadcasted_iota(I32, (1, ch), 1)
    _init_state(state)

    def step(masked, c, carry):
        off = pl.multiple_of(c * ch, ch)
        kc = kv_ref[0, pl.ds(off, ch), :]
        _online_update(_dot_nt(q, kc), (off + lane <= qpos) if masked else None, kc[:, :LANES], state)
        return carry

    _causal_loops(step, q0, tq)
    o_ref[0, 0] = _finish(state).astype(BF16)


def _mla(q, kv, tq, past):
    b, nq, rows, f = q.shape
    s_pad = kv.shape[1]
    n_once = _once_len(nq, rows, past, tq)
    return pl.pallas_call(
        functools.partial(_mla_kernel, tq=tq, past=past, n_once=n_once),
        grid=(b, nq),
        in_specs=[pl.BlockSpec((1, 1, rows, f), lambda i, j: (i, j, 0, 0)),
                  pl.BlockSpec((1, s_pad, f), lambda i, j: (i, 0, 0))],
        out_specs=pl.BlockSpec((1, 1, rows, LANES), lambda i, j: (i, j, 0, 0)),
        out_shape=jax.ShapeDtypeStruct((b, nq, rows, LANES), BF16),
        scratch_shapes=_state_scratch(rows, 0 if n_once else 1),
        compiler_params=_params("arbitrary", "arbitrary"), name="mla",
    )(q, kv)


def _mlao_kernel(ol_ref, wuv_ref, o_ref):
    for jp in range(B_HEADS // 2):
        o = _dot(ol_ref[0, 2 * jp], wuv_ref[2 * jp]) + _dot(ol_ref[0, 2 * jp + 1], wuv_ref[2 * jp + 1])
        o_ref[:, jp * LANES:(jp + 1) * LANES] = o.astype(BF16)


def _mlao(olat, wuv):
    nb = olat.shape[0]
    return pl.pallas_call(
        _mlao_kernel,
        grid=(nb,),
        in_specs=[pl.BlockSpec((1, B_HEADS, QBLK, LANES), lambda i: (i, 0, 0, 0)),
                  pl.BlockSpec((B_HEADS, LANES, LANES), lambda i: (0, 0, 0))],
        out_specs=pl.BlockSpec((QBLK, B_HEADS * B_VDH), lambda i: (i, 0)),
        out_shape=jax.ShapeDtypeStruct((nb * QBLK, B_HEADS * B_VDH), BF16),
        compiler_params=_params("arbitrary"), name="mlao",
    )(olat, wuv)


def _compress_kernel(x_ref, pe_ref, phi_ref, o_ref):
    phi = phi_ref[...]
    bias = _dot(pe_ref[...], phi)
    o_ref[0] = _dot(x_ref[0], phi) + bias[0:1]


def _compress(x, pe_rows, phi_big, n_blk):
    b = x.shape[0]
    f = x.shape[2]
    return pl.pallas_call(
        _compress_kernel,
        grid=(b,),
        in_specs=[pl.BlockSpec((1, n_blk, f), lambda i: (i, 0, 0)),
                  pl.BlockSpec((8, f), lambda i: (0, 0)),
                  pl.BlockSpec((f, LANES), lambda i: (0, 0))],
        out_specs=pl.BlockSpec((1, n_blk, LANES), lambda i: (i, 0, 0)),
        out_shape=jax.ShapeDtypeStruct((b, n_blk, LANES), F32),
        compiler_params=_params("arbitrary"), name="compress",
    )(x, pe_rows, phi_big)


def _nsa_kernel(qc_ref, qr_ref, gate_ref, cmp_ref, sel_ref, win_ref, o_ref, j_ref, *state,
                tq, past, win_base, wl, nsel_pad, n_top, n_once):
    ch = KEY_CHUNK
    heads = C_HEADS
    rows = heads * tq
    q0 = past + pl.program_id(1) * tq
    qpos_t = q0 + lax.broadcasted_iota(I32, (tq, 1), 0)
    qpos_r = _row_pos(rows, tq, q0)
    qc = qc_ref[0, 0]
    qr = qr_ref[0, 0]

    kvc = cmp_ref[0]
    nb = kvc.shape[0]
    nbh = nb // 2
    pcol = lax.broadcasted_iota(I32, (1, nb), 1)
    blk = jnp.where(pcol < nbh, 2 * pcol, 2 * (pcol - nbh) + 1)
    cmask = (blk + 1) * CMP_BLK - 1 <= qpos_r
    s_c = jnp.where(cmask, _dot_nt(qc, kvc), NEG)
    e_c = jnp.where(cmask, jnp.exp(s_c - jnp.max(s_c, axis=1, keepdims=True)), 0.0)
    p_c = e_c / jnp.maximum(jnp.sum(e_c, axis=1, keepdims=True), 1e-30)
    o_c = _dot(p_c.astype(BF16), kvc)

    imp = p_c[0:tq]
    for h in range(1, heads):
        imp = imp + p_c[h * tq:(h + 1) * tq]
    imp = imp[:, :nbh] + imp[:, nbh:]
    if nsel_pad > nbh:
        imp = jnp.concatenate([imp, jnp.zeros((tq, nsel_pad - nbh), F32)], axis=1)
    bidx = lax.broadcasted_iota(I32, (tq, nsel_pad), 1)
    forced = (bidx == (qpos_t >> 6)) | (bidx == 0)
    visible = bidx * SEL_BLK <= qpos_t
    imp = jnp.where(visible, jnp.where(forced, FORCE_SCORE, imp), -jnp.inf)
    keys = _sort_key(imp)
    t, j = _topk_threshold(lambda c: keys, 1, nsel_pad, tq, n_top, int(math.log2(nsel_pad)), j_ref)
    bsel = _selected(keys, bidx, t, j).astype(BF16)

    def key_mask(off, width):
        erow = lax.broadcasted_iota(I32, (nsel_pad, width), 0)
        ecol = (off + lax.broadcasted_iota(I32, (nsel_pad, width), 1)) >> 6
        ksel = _dot(bsel, jnp.where(erow == ecol, 1.0, 0.0).astype(BF16))
        visible = off + lax.broadcasted_iota(I32, (tq, width), 1) <= qpos_t
        return _tile_rows(jnp.where(visible, ksel, 0.0), heads) > 0.5

    if n_once:
        kv = sel_ref[0, :n_once, :]
        o_s = _attend_once(_dot_nt(qr, kv), key_mask(0, n_once), kv)
    else:
        _init_state(state)

        def step(c, carry):
            off = pl.multiple_of(c * ch, ch)
            kv = sel_ref[0, pl.ds(off, ch), :]
            _online_update(_dot_nt(qr, kv), key_mask(off, ch), kv, state)
            return carry

        lax.fori_loop(0, (q0 + tq - 1) // ch + 1, step, 0)
        o_s = _finish(state)

    w_rows = win_ref.shape[1]
    start = jnp.clip(q0 - WINDOW - win_base, 0, w_rows - wl)
    start = pl.multiple_of(start, LANES)
    kw = win_ref[0, pl.ds(start, wl), :]
    dist = qpos_t - (win_base + start + lax.broadcasted_iota(I32, (tq, wl), 1))
    in_win = jnp.where(dist >= 0, jnp.where(dist < WINDOW, 1.0, 0.0), 0.0)
    o_w = _attend_once(_dot_nt(qr, kw), _tile_rows(in_win, heads) > 0.5, kw)

    gates = gate_ref[0]
    col = lambda k: jnp.concatenate([gates[:, 3 * h + k:3 * h + k + 1] for h in range(heads)], axis=0)
    o_ref[0, 0] = (col(0) * o_c + col(1) * o_s + col(2) * o_w).astype(BF16)


def _nsa(qc, qr, gates, kvcmp, kvsel, kvwin, tq, past, win_base, s_total):
    b, nq, rows, _ = qc.shape
    nb = kvcmp.shape[1]
    s_pad = kvsel.shape[1]
    w_rows = kvwin.shape[1]
    wl = min(w_rows, -(-(WINDOW + tq) // LANES) * LANES)
    n_sel = -(-s_total // SEL_BLK)
    nsel_pad = -(-n_sel // LANES) * LANES
    n_once = _once_len(nq, rows, past, tq)
    return pl.pallas_call(
        functools.partial(_nsa_kernel, tq=tq, past=past, win_base=win_base, wl=wl,
                          nsel_pad=nsel_pad, n_top=min(N_SEL, n_sel), n_once=n_once),
        grid=(b, nq),
        in_specs=[pl.BlockSpec((1, 1, rows, LANES), lambda i, j: (i, j, 0, 0)),
                  pl.BlockSpec((1, 1, rows, LANES), lambda i, j: (i, j, 0, 0)),
                  pl.BlockSpec((1, tq, LANES), lambda i, j: (i, j, 0)),
                  pl.BlockSpec((1, nb, LANES), lambda i, j: (i, 0, 0)),
                  pl.BlockSpec((1, s_pad, LANES), lambda i, j: (i, 0, 0)),
                  pl.BlockSpec((1, w_rows, LANES), lambda i, j: (i, 0, 0))],
        out_specs=pl.BlockSpec((1, 1, rows, LANES), lambda i, j: (i, j, 0, 0)),
        out_shape=jax.ShapeDtypeStruct((b, nq, rows, LANES), BF16),
        scratch_shapes=[pltpu.VMEM((tq, LANES), I32)] + _state_scratch(rows, 0 if n_once else 1),
        compiler_params=_params("arbitrary", "arbitrary"), name="nsa",
    )(qc, qr, gates, kvcmp, kvsel, kvwin)


def _diff_kernel(q_ref, kv_ref, lam_ref, sub_ref, o_ref, *scr, tq, past, lam_init, n_once):
    ch = KEY_CHUNK
    rep = D_HEADS // D_KVH
    rows = rep * tq
    q0 = past + pl.program_id(1) * tq
    qpos = _row_pos(rows, tq, q0)
    outs = [None] * (2 * D_KVH)

    if n_once:
        mask = lax.broadcasted_iota(I32, (1, n_once), 1) <= qpos
        for g in range(D_KVH):
            kk = kv_ref[0, :n_once, g * LANES:(g + 1) * LANES]
            vv = kv_ref[0, :n_once, (D_KVH + g) * LANES:(D_KVH + g + 1) * LANES]
            for m in range(2):
                idx = g * 2 + m
                outs[idx] = _attend_once(_dot_nt(q_ref[0, 0, idx * rows:(idx + 1) * rows, :], kk), mask, vv)
    else:
        lane = lax.broadcasted_iota(I32, (1, ch), 1)
        states = [scr[3 * k:3 * k + 3] for k in range(2 * D_KVH)]
        for st in states:
            _init_state(st)

        def step(masked, c, carry):
            off = pl.multiple_of(c * ch, ch)
            mask = (off + lane <= qpos) if masked else None
            kv = kv_ref[0, pl.ds(off, ch), :]
            for g in range(D_KVH):
                kk = kv[:, g * LANES:(g + 1) * LANES]
                vv = kv[:, (D_KVH + g) * LANES:(D_KVH + g + 1) * LANES]
                for m in range(2):
                    idx = g * 2 + m
                    _online_update(_dot_nt(q_ref[0, 0, idx * rows:(idx + 1) * rows, :], kk), mask, vv, states[idx])
            return carry

        _causal_loops(step, q0, tq)
        outs = [_finish(st) for st in states]

    lv = lam_ref[...]
    lam = (jnp.exp(jnp.sum(lv[0:1] * lv[1:2], axis=1, keepdims=True))
           - jnp.exp(jnp.sum(lv[2:3] * lv[3:4], axis=1, keepdims=True)) + lam_init)
    for g in range(D_KVH):
        o = outs[2 * g] - lam * outs[2 * g + 1]
        o = _rms(o, sub_ref[...]) * (1.0 - lam_init)
        for r in range(rep):
            head = g * rep + r
            o_ref[0, :, head * LANES:(head + 1) * LANES] = o[r * tq:(r + 1) * tq]


def _diff(q, kv, lam_vec, subln, tq, past, lam_init):
    b, nq, qrows, _ = q.shape
    s_pad = kv.shape[1]
    rows = (D_HEADS // D_KVH) * tq
    n_once = _once_len(nq, rows, past, tq)
    return pl.pallas_call(
        functools.partial(_diff_kernel, tq=tq, past=past, lam_init=lam_init, n_once=n_once),
        grid=(b, nq),
        in_specs=[pl.BlockSpec((1, 1, qrows, LANES), lambda i, j: (i, j, 0, 0)),
                  pl.BlockSpec((1, s_pad, 4 * LANES), lambda i, j: (i, 0, 0)),
                  pl.BlockSpec((4, D_DH), lambda i, j: (0, 0)),
                  pl.BlockSpec((1, 2 * D_DH), lambda i, j: (0, 0))],
        out_specs=pl.BlockSpec((1, tq, D_HEADS * 2 * D_DH), lambda i, j: (i, j, 0)),
        out_shape=jax.ShapeDtypeStruct((b, nq * tq, D_HEADS * 2 * D_DH), F32),
        scratch_shapes=_state_scratch(rows, 0 if n_once else 2 * D_KVH),
        compiler_params=_params("arbitrary", "arbitrary"), name="diff",
    )(q, kv, lam_vec, subln)


def _route(logits):
    lane = lax.broadcasted_iota(I32, logits.shape, 1)
    lane_f = lane.astype(F32)
    big = 1e9
    is_g = lane < N_GROUPS
    gl = jnp.where(is_g, logits, -jnp.inf)
    gmax = jnp.max(gl, axis=1, keepdims=True)
    grp = jnp.min(jnp.where(gl == gmax, lane_f, big), axis=1, keepdims=True)
    g_w = 1.0 / jnp.sum(jnp.where(is_g, jnp.exp(gl - gmax), 0.0), axis=1, keepdims=True)
    e_grp = ((lane - N_GROUPS) >> 2).astype(F32)
    el = jnp.where(e_grp == grp, logits, -jnp.inf)
    el = jnp.where(lane >= N_GROUPS, jnp.where(lane < N_GROUPS + N_EXPERTS, el, -jnp.inf), -jnp.inf)
    v1 = jnp.max(el, axis=1, keepdims=True)
    i1 = jnp.min(jnp.where(el == v1, lane_f, big), axis=1, keepdims=True)
    el2 = jnp.where(lane_f == i1, -jnp.inf, el)
    v2 = jnp.max(el2, axis=1, keepdims=True)
    i2 = jnp.min(jnp.where(el2 == v2, lane_f, big), axis=1, keepdims=True)
    e21 = jnp.exp(v2 - v1)
    w1 = g_w / (1.0 + e21)
    return jnp.where(lane_f == i1, w1, 0.0) + jnp.where(lane_f == i2, w1 * e21, 0.0)


def _moe_kernel(ma_ref, mb_ref, x_ref, m2_ref, m3_ref, m4_ref, m5_ref, g_ref, woa_ref, wob_ref,
                wrh_ref, wrl_ref, wg_ref, wu_ref, wd_ref, gf_ref, o_ref,
                x1_ref, h_ref, comb_ref, acc_ref, *, final_norm):
    bb, tt, d = x_ref.shape
    tm = bb * tt
    e = pl.program_id(2)

    @pl.when(e == 0)
    def _():
        mix = _dot(ma_ref[...], woa_ref[...]) + _dot(mb_ref[...], wob_ref[...])
        x1 = x_ref[...] + m2_ref[...] * mix.reshape(bb, tt, d)
        h = (_rms(x1, g_ref[...]) * (1.0 + m4_ref[...]) + m3_ref[...]).reshape(tm, d)
        x1_ref[...] = x1.reshape(tm, d)
        h_hi = h.astype(BF16)
        h_lo = (h - h_hi.astype(F32)).astype(BF16)
        h_ref[...] = h_hi
        logits = _dot(h_hi, wrh_ref[...]) + (_dot(h_hi, wrl_ref[...]) + _dot(h_lo, wrh_ref[...]))
        comb_ref[...] = _route(logits)
        acc_ref[...] = jnp.zeros(acc_ref.shape, F32)

    hb = h_ref[...]
    a = _dot(hb, wg_ref[0])
    u = _dot(hb, wu_ref[0])
    lane = lax.broadcasted_iota(I32, (tm, LANES), 1)
    ce = jnp.sum(jnp.where(lane == e + N_GROUPS, comb_ref[...], 0.0), axis=1, keepdims=True)
    act = (a * _sigmoid(a)) * u * ce
    acc_ref[...] += _dot(act.astype(BF16), wd_ref[0])

    @pl.when(e == pl.num_programs(2) - 1)
    def _():
        out = x1_ref[...].reshape(bb, tt, d) + m5_ref[...] * acc_ref[...].reshape(bb, tt, d)
        if final_norm:
            out = _rms(out, gf_ref[...])
        o_ref[...] = out


def _moe(mix_a, mix_b, x, mod, g_ffn, woa, wob, wrh, wrl, wg, wu, wd, g_final, final_norm, bb, tt):
    bx, tx, d = x.shape
    tm = bb * tt
    blk = lambda n: pl.BlockSpec((bb, tt, n), lambda i, j, e: (i, j, 0))
    nj = tx // tt
    flat = lambda a: pl.BlockSpec((tm, a.shape[-1]), lambda i, j, e: (i * nj + j, 0))
    modspec = pl.BlockSpec((bb, 1, d), lambda i, j, e: (i, 0, 0))
    vec = pl.BlockSpec((1, 1, d), lambda i, j, e: (0, 0, 0))
    full = lambda a: pl.BlockSpec(a.shape, lambda i, j, e: (0,) * a.ndim)
    n_e, _, f = wg.shape
    return pl.pallas_call(
        functools.partial(_moe_kernel, final_norm=final_norm),
        grid=(bx // bb, tx // tt, n_e),
        in_specs=[flat(mix_a), flat(mix_b), blk(d), modspec, modspec, modspec, modspec, vec,
                  full(woa), full(wob), full(wrh), full(wrl),
                  pl.BlockSpec((1, d, f), lambda i, j, e: (e, 0, 0)),
                  pl.BlockSpec((1, d, f), lambda i, j, e: (e, 0, 0)),
                  pl.BlockSpec((1, f, d), lambda i, j, e: (e, 0, 0)), vec],
        out_specs=blk(d),
        out_shape=jax.ShapeDtypeStruct((bx, tx, d), F32),
        scratch_shapes=[pltpu.VMEM((tm, d), F32), pltpu.VMEM((tm, d), BF16),
                        pltpu.VMEM((tm, LANES), F32), pltpu.VMEM((tm, d), F32)],
        compiler_params=_params("arbitrary", "arbitrary", "arbitrary"), name="moe",
    )(mix_a.reshape(bx * tx, -1), mix_b.reshape(bx * tx, -1), x, mod[2], mod[3], mod[4], mod[5], g_ffn.reshape(1, 1, d),
      woa, wob, wrh, wrl, wg, wu, wd, g_final.reshape(1, 1, d))


def _head_major(x, nq, tq):
    b, _, h, f = x.shape
    return x.reshape(b, nq, tq, h, f).transpose(0, 1, 3, 2, 4).reshape(b, nq, h * tq, f)


def _token_major(x, tq, h):
    b, nq, _, f = x.shape
    return x.reshape(b, nq, h, tq, f).transpose(0, 1, 3, 2, 4).reshape(b, nq * tq, h, f)


def _lane_pad(x, slot, n_slots=2):
    parts = [jnp.where(jnp.asarray(np.asarray(slot) == s)[:, None], x, 0) for s in range(n_slots)]
    return jnp.concatenate(parts, axis=-1)


def _even_mixer(proj, kv_rows, mla_w, tabs, tab_blocks, tq, past, s_total):
    b, t, _ = proj.shape
    nq = t // tq
    ki, akv, kmla = kv_rows
    qi = _head_major(proj[..., EV_QI:EV_QI + 512].astype(BF16).reshape(b, t, IDX_HEADS, IDX_DH), nq, tq)
    wi = proj[..., EV_WI:EV_WI + IDX_HEADS] * (IDX_HEADS ** -0.5 * IDX_DH ** -0.5)
    qa = (proj[..., EV_QA:EV_QA + 512] * A_DH ** -0.5).astype(BF16).reshape(b, t, A_HEADS, A_DH)
    qa = _head_major(_lane_pad(qa, np.arange(A_HEADS) // (A_HEADS // A_KVH)), nq, tq)
    oa = _dsa(qi, wi, qa, ki, akv, tq, past, s_total)
    oa = _token_major(oa, tq, A_HEADS).reshape(b, t, A_HEADS, A_KVH, A_DH)
    hsel = jnp.asarray(np.arange(A_HEADS) // (A_HEADS // A_KVH))
    oa = jnp.take_along_axis(oa, hsel[None, None, :, None, None], axis=3).reshape(b, t, A_HEADS * A_DH)

    q_norm, wn, wr, wuk, wuv = mla_w
    qp = _mlaq(proj[..., EV_QLAT:EV_QLAT + B_QLORA].reshape(b * t, B_QLORA), q_norm, wn, wr, wuk, tabs, tab_blocks)
    if tq == QBLK:
        qp = qp.reshape(b, nq, B_HEADS * tq, 2 * LANES)
    else:
        per = QBLK // tq
        qp = qp.reshape(b // per, B_HEADS, per, tq, 2 * LANES).transpose(0, 2, 1, 3, 4).reshape(b, 1, B_HEADS * tq, 2 * LANES)
    ol = _mla(qp, kmla, tq, past)
    if tq == QBLK:
        ol = ol.reshape(b * nq, B_HEADS, tq, LANES)
    else:
        per = QBLK // tq
        ol = ol.reshape(b // per, per, B_HEADS, tq, LANES).transpose(0, 2, 1, 3, 4).reshape(b // per, B_HEADS, QBLK, LANES)
    ob = _mlao(ol, wuv).reshape(b, t, B_HEADS * B_VDH)
    return oa, ob


def _odd_mixer(proj, kv_rows, cmp_w, diff_w, tq, past, win_base, s_total, lam_init):
    b, t, _ = proj.shape
    nq = t // tq
    cmp_rows, kvsel, kvwin, dkv = kv_rows
    pe_rows, phi_big = cmp_w
    n_cmp = s_total // CMP_BLK
    kvc = _compress(cmp_rows, pe_rows, phi_big, n_cmp)
    nb = -(-n_cmp // (2 * LANES)) * 2 * LANES
    kvc = jnp.pad(kvc.astype(BF16), ((0, 0), (0, nb - n_cmp), (0, 0)))
    kvc = jnp.concatenate([kvc[:, 0::2], kvc[:, 1::2]], axis=1)
    zeros_slot = np.zeros(C_HEADS, np.int64)
    qc = (proj[..., OD_QC:OD_QC + 512] * C_DH ** -0.5).astype(BF16).reshape(b, t, C_HEADS, C_DH)
    qr = (proj[..., OD_QCR:OD_QCR + 512] * C_DH ** -0.5).astype(BF16).reshape(b, t, C_HEADS, C_DH)
    qc = _head_major(_lane_pad(qc, zeros_slot), nq, tq)
    qr = _head_major(_lane_pad(qr, zeros_slot), nq, tq)
    gates = proj[..., OD_GATE:OD_GATE + LANES]
    oc = _nsa(qc, qr, gates, kvc, kvsel, kvwin, tq, past, win_base, s_total)
    oc = _token_major(oc, tq, C_HEADS)[..., C_DH:].reshape(b, t, C_HEADS * C_DH)

    lam_vec, subln = diff_w
    qd = (proj[..., OD_QD:OD_QD + 512] * D_DH ** -0.5).astype(BF16)
    rep = D_HEADS // D_KVH
    qd = qd.reshape(b, t, D_KVH, rep, 2, D_DH).transpose(0, 1, 2, 4, 3, 5)
    qd = qd.reshape(b, t, D_KVH * 2 * rep, D_DH)
    slot = np.tile(np.repeat(np.arange(2), rep), D_KVH)
    qd = _head_major(_lane_pad(qd, slot), nq, tq)
    od = _diff(qd, dkv, lam_vec, subln, tq, past, lam_init)
    return oc, od.astype(BF16)


def kernel(x_prompt, x_sample, c_prompt, c_sample, cache_a_kv, cache_a_idx, cache_b_latent, cache_b_krope,
           cache_c_cmp, cache_c_sel, state_c_win, cache_d_kv, page_table, w_ada, b_ada, g_mix, g_ffn, w_out,
           w_in_even, b_q_norm, b_kv_norm, w_b_qb, w_b_kvb, w_in_odd, c_phi_k, c_phi_v, c_pe_k, c_pe_v,
           d_lambda, d_subln, moe_w_group, moe_w_expert, moe_w_gate, moe_w_up, moe_w_down, g_final):
    depth = w_ada.shape[0]
    bp, tp, d = x_prompt.shape
    bs, ts, _ = x_sample.shape
    n_pages = page_table.shape[1]
    past = n_pages * PAGE_SIZE
    win_buf = state_c_win.shape[2]
    n_pool = cache_a_kv.shape[1]
    assert tp % KEY_CHUNK == 0 and QBLK % ts == 0 and (bs * ts) % QBLK == 0

    bc = -(-(bp + bs) // 8) * 8
    c_all = jnp.pad(jnp.concatenate([c_prompt, c_sample]), ((0, bc - bp - bs), (0, 0)))
    mod_all = _ada(c_all, w_ada, b_ada).reshape(depth, bc, 6, 1, d)

    pos_p = jnp.arange(tp, dtype=I32)
    pos_s = past + jnp.arange(ts, dtype=I32)
    tabs_p = _all_tables(pos_p)
    bb_s = min(bs, 64)
    tabs_s = jnp.tile(_all_tables(pos_s), (1, max(bb_s, QBLK // ts), 1))

    xp, xs = x_prompt, x_sample
    rows = {k: [] for k in ("akv_p", "akv_s", "idx_p", "idx_s", "lat_p", "lat_s", "kr_p", "kr_s",
                            "cmp_p", "cmp_s", "sel_p", "sel_s", "win_p", "win_s", "dkv_p", "dkv_s")}
    tm_p = 512
    for l in range(depth):
        li = l // 2
        mod_p = [mod_all[l, :bp, k] for k in range(6)]
        mod_s = [mod_all[l, bp:bp + bs, k] for k in range(6)]
        if l % 2 == 0:
            w, groups = _even_weights(w_in_even[li])
            kvn = b_kv_norm[li].reshape(1, LANES)
            pp = _proj(xp, mod_p[0], mod_p[1], g_mix[l], w, tabs_p, kvn, groups, EV_OUT, 1, tm_p)
            ps = _proj(xs, mod_s[0], mod_s[1], g_mix[l], w, tabs_s, kvn, groups, EV_OUT, bb_s, ts)
            wqb = w_b_qb[li]
            pad_h = lambda a: jnp.pad(a, ((0, 0), (0, 0), (0, LANES - a.shape[2]))).reshape(a.shape[0], -1)
            wn = pad_h(wqb[..., :B_NOPE]).astype(BF16)
            wr = pad_h(wqb[..., B_NOPE:]).astype(BF16)
            wkvb = w_b_kvb[li]
            wuk = jnp.pad(wkvb[..., :B_NOPE].transpose(1, 2, 0), ((0, 0), (0, LANES - B_NOPE), (0, 0))).astype(BF16)
            wuv = wkvb[..., B_NOPE:].transpose(1, 0, 2)
            wuv = _lane_pad(wuv.transpose(1, 0, 2), np.arange(B_HEADS) % 2).transpose(1, 0, 2).astype(BF16)
            mla_w = (b_q_norm[li].reshape(1, B_QLORA), wn, wr, wuk, wuv)

            def kv_prompt(p):
                kmla = jnp.concatenate([p[..., EV_CKV:EV_CKV + LANES], p[..., EV_KR:EV_KR + B_ROPE],
                                        jnp.zeros(p.shape[:2] + (LANES - B_ROPE,), F32)], axis=-1)
                return (p[..., EV_KI:EV_KI + IDX_DH].astype(BF16), p[..., EV_AKV:EV_AKV + 256].astype(BF16),
                        kmla.astype(BF16))

            oa_p, ob_p = _even_mixer(pp, kv_prompt(pp), mla_w, tabs_p, tp // QBLK, QBLK, 0, tp)
            new_s = (ps[..., EV_KI:EV_KI + IDX_DH], ps[..., EV_AKV:EV_AKV + 256],
                     ps[..., EV_CKV:EV_CKV + LANES], ps[..., EV_KR:EV_KR + B_ROPE])
            ki_s = _gather(page_table, [cache_a_idx], li, [new_s[0]], IDX_DH)
            akv_s = _gather(page_table, [cache_a_kv], li, [new_s[1]], 256)
            kmla_s = _gather(page_table, [cache_b_latent, cache_b_krope], li, [new_s[2], new_s[3]], 2 * LANES)
            oa_s, ob_s = _even_mixer(ps, (ki_s, akv_s, kmla_s), mla_w, tabs_s, 1, ts, past, past + ts)
            for tag, p in (("p", pp), ("s", ps)):
                rows["akv_" + tag].append(p[..., EV_AKV:EV_AKV + 256].reshape(p.shape[:2] + (2, A_KVH, A_DH)))
                rows["idx_" + tag].append(p[..., EV_KI:EV_KI + IDX_DH])
                rows["lat_" + tag].append(p[..., EV_CKV:EV_CKV + B_KVLORA])
                rows["kr_" + tag].append(p[..., EV_KR:EV_KR + B_ROPE])
            mix_p, mix_s = (oa_p, ob_p), (oa_s, ob_s)
        else:
            w, groups = _odd_weights(w_in_odd[li])
            kvn = jnp.ones((1, LANES), F32)
            pp = _proj(xp, mod_p[0], mod_p[1], g_mix[l], w, tabs_p, kvn, groups, OD_OUT, 1, tm_p)
            ps = _proj(xs, mod_s[0], mod_s[1], g_mix[l], w, tabs_s, kvn, groups, OD_OUT, bb_s, ts)
            lam_init = 0.8 - 0.6 * math.exp(-0.3 * l)
            zk = jnp.zeros_like(c_phi_k[li])
            phi_big = jnp.concatenate([jnp.stack([c_phi_k[li], zk], axis=1), jnp.stack([zk, c_phi_v[li]], axis=1)],
                                      axis=-1).reshape(CMP_BLK * 2 * C_DH, 2 * C_DH).astype(BF16)
            pe = jnp.stack([c_pe_k[li], c_pe_v[li]], axis=1).reshape(1, CMP_BLK * 2 * C_DH)
            cmp_w = (jnp.tile(pe, (8, 1)).astype(BF16), phi_big)
            diff_w = (d_lambda[li], d_subln[li].reshape(1, 2 * D_DH))
            blk_rows = lambda a: a.reshape(a.shape[0], a.shape[1] // CMP_BLK, CMP_BLK * 2 * C_DH)
            kv_p = (blk_rows(pp[..., OD_CMP:OD_CMP + LANES].astype(BF16)),
                    pp[..., OD_SEL:OD_SEL + LANES].astype(BF16), pp[..., OD_WIN:OD_WIN + LANES].astype(BF16),
                    pp[..., OD_DKV:OD_DKV + 512].astype(BF16))
            oc_p, od_p = _odd_mixer(pp, kv_p, cmp_w, diff_w, QBLK, 0, 0, tp, lam_init)
            flat = lambda a: a.reshape(a.shape[:3] + (-1,))
            new_cmp, new_sel = ps[..., OD_CMP:OD_CMP + LANES], ps[..., OD_SEL:OD_SEL + LANES]
            new_win, new_dkv = ps[..., OD_WIN:OD_WIN + LANES], ps[..., OD_DKV:OD_DKV + 512]
            cmp_s = _gather(page_table, [cache_c_cmp], li, [new_cmp], LANES)
            sel_s = _gather(page_table, [cache_c_sel], li, [new_sel], LANES)
            dkv_s = _gather(page_table, [cache_d_kv], li, [new_dkv], 4 * LANES)
            win_full = jnp.concatenate([state_c_win[li].reshape(bs, win_buf, LANES), new_win], axis=1)
            w_rows = -(-(win_buf + ts) // LANES) * LANES
            win_s = jnp.pad(win_full, ((0, 0), (0, w_rows - win_buf - ts), (0, 0))).astype(BF16)
            oc_s, od_s = _odd_mixer(ps, (blk_rows(cmp_s), sel_s, win_s, dkv_s), cmp_w, diff_w,
                                    ts, past, past - win_buf, past + ts, lam_init)
            keep_p = min(WINDOW, tp)
            rows["win_p"].append(pp[:, tp - keep_p:, OD_WIN:OD_WIN + LANES].reshape(bp, keep_p, 2, C_DH))
            rows["win_s"].append(win_full[:, ts:].reshape(bs, win_buf, 2, C_DH))
            for tag, p in (("p", pp), ("s", ps)):
                rows["cmp_" + tag].append(p[..., OD_CMP:OD_CMP + LANES].reshape(p.shape[:2] + (2, C_DH)))
                rows["sel_" + tag].append(p[..., OD_SEL:OD_SEL + LANES].reshape(p.shape[:2] + (2, C_DH)))
                rows["dkv_" + tag].append(p[..., OD_DKV:OD_DKV + 512].reshape(p.shape[:2] + (2, D_KVH, 2 * D_DH)))
            mix_p, mix_s = (oc_p, od_p), (oc_s, od_s)

        woa, wob = w_out[l, :512].astype(BF16), w_out[l, 512:].astype(BF16)
        wr_full = jnp.concatenate([moe_w_group[l], moe_w_expert[l],
                                   jnp.zeros((d, LANES - N_GROUPS - N_EXPERTS), F32)], axis=1)
        wrh = wr_full.astype(BF16)
        wrl = (wr_full - wrh.astype(F32)).astype(BF16)
        wg, wu, wd = moe_w_gate[l].astype(BF16), moe_w_up[l].astype(BF16), moe_w_down[l].astype(BF16)
        last = l == depth - 1
        xp = _moe(mix_p[0], mix_p[1], xp, mod_p, g_ffn[l], woa, wob, wrh, wrl, wg, wu, wd, g_final, last,
                  1, min(tp, 1024))
        xs = _moe(mix_s[0], mix_s[1], xs, mod_s, g_ffn[l], woa, wob, wrh, wrl, wg, wu, wd, g_final, last,
                  bs, ts)

    st = lambda k: jnp.stack(rows[k])
    return (xp, xs, st("akv_p"), st("akv_s"), st("idx_p"), st("idx_s"), st("lat_p"), st("lat_s"),
            st("kr_p"), st("kr_s"), st("cmp_p"), st("cmp_s"), st("sel_p"), st("sel_s"),
            st("win_p"), st("win_s"), st("dkv_p"), st("dkv_s"))
```

```python
import functools
import math

import jax
import jax.numpy as jnp
import numpy as np
from jax import lax
from jax.experimental import pallas as pl
from jax.experimental.pallas import tpu as pltpu

F32 = jnp.float32
BF16 = jnp.bfloat16
I32 = jnp.int32

D_MODEL = 1024
PAGE_SIZE = 128
QBLK = 128
ROPE_THETA = 10000.0
EPS = 1e-6
NEG = -1e30
FORCE_SCORE = 1e9

A_HEADS, A_DH, A_KVH = 8, 64, 2
IDX_HEADS, IDX_DH, IDX_TOPK_MAX = 8, 64, 256
B_HEADS, B_NOPE, B_ROPE, B_VDH, B_QLORA, B_KVLORA = 8, 64, 32, 64, 256, 128
C_HEADS, C_DH, CMP_BLK, SEL_BLK, N_SEL, WINDOW = 8, 64, 32, 64, 16, 512
D_HEADS, D_DH, D_KVH = 4, 64, 2
N_GROUPS, E_PER_GROUP, E_HIDDEN = 4, 4, 256
N_EXPERTS = N_GROUPS * E_PER_GROUP

LANES = 128
KEY_CHUNK = 512
DIFF_CHUNK = 1024
GATHER_PAGES = 8
VMEM_LIMIT = 56 * 1024 * 1024

EV_QA, EV_QI, EV_AKV, EV_KI, EV_CKV, EV_QLAT, EV_KR = 0, 512, 1024, 1280, 1408, 1536, 1792
EV_WI = EV_KR + B_ROPE
EV_OUT = 1920
OD_QC, OD_QCR, OD_QD, OD_DKV, OD_CMP, OD_SEL, OD_WIN, OD_GATE = 0, 512, 1024, 1536, 2048, 2176, 2304, 2432
OD_OUT = 2560


def _dot(a, b):
    return jnp.dot(a, b, preferred_element_type=F32)


def _dot_nt(a, b):
    return lax.dot_general(a, b, (((1,), (1,)), ((), ())), preferred_element_type=F32)


def _params(*sem):
    return pltpu.CompilerParams(dimension_semantics=sem, vmem_limit_bytes=VMEM_LIMIT)


def _rms(x, g):
    return x * lax.rsqrt(jnp.mean(x * x, axis=-1, keepdims=True) + EPS) * g


def _sigmoid(x):
    return 1.0 / (1.0 + jnp.exp(-x))


def _ada_kernel(c_ref, w_ref, b_ref, o_ref):
    c = c_ref[...]
    s = (c * _sigmoid(c)).astype(BF16)
    o_ref[0] = _dot(s, w_ref[0].astype(BF16)) + b_ref[0]


def _ada(c_all, w_ada, b_ada):
    depth, d, n = w_ada.shape
    bc = c_all.shape[0]
    nb = 1536
    return pl.pallas_call(
        _ada_kernel,
        grid=(depth, n // nb),
        in_specs=[pl.BlockSpec((bc, d), lambda l, j: (0, 0)),
                  pl.BlockSpec((1, d, nb), lambda l, j: (l, 0, j)),
                  pl.BlockSpec((1, 1, nb), lambda l, j: (l, 0, j))],
        out_specs=pl.BlockSpec((1, bc, nb), lambda l, j: (l, 0, j)),
        out_shape=jax.ShapeDtypeStruct((depth, bc, n), F32),
        compiler_params=_params("arbitrary", "arbitrary"), name="ada",
    )(c_all, w_ada, b_ada.reshape(depth, 1, n))


def _rope_tables(pos, width, active):
    half = width // 2
    inv = ROPE_THETA ** (-jnp.arange(half, dtype=F32) / half)
    ang = pos.astype(F32)[:, None] * inv[None, :]
    cos, sin = jnp.cos(ang), jnp.sin(ang)
    lane = np.arange(LANES)
    j = lane % width
    cos_l = cos[:, j % half]
    sin_l = sin[:, j % half] * jnp.asarray(np.where(j < half, -1.0, 1.0), F32)[None, :]
    act = jnp.asarray(lane < active)[None, :]
    return jnp.where(act, cos_l, 1.0), jnp.where(act, sin_l, 0.0)


def _all_tables(pos):
    t = []
    for width, active in ((64, 128), (64, 64), (32, 32)):
        t.extend(_rope_tables(pos, width, active))
    return jnp.stack(t)


TAB = {"rope64": 0, "rope64h": 2, "rope32": 4}
WIDTH = {"rope64": 64, "rope64h": 64, "rope32": 32}


def _rope(y, cos, sin, width):
    half = width // 2
    lane = lax.broadcasted_iota(I32, y.shape, 1)
    first = (lane & (width - 1)) < half
    up = pltpu.roll(y, LANES - half, axis=1)
    dn = pltpu.roll(y, half, axis=1)
    return y * cos + jnp.where(first, up, dn) * sin


def _proj_kernel(x_ref, sh_ref, sc_ref, g_ref, w_ref, tab_ref, kvn_ref, o_ref, *, chunks, groups):
    bb, tt, d = x_ref.shape
    h = _rms(x_ref[...], g_ref[...]) * (1.0 + sc_ref[...]) + sh_ref[...]
    hb = h.reshape(bb * tt, d).astype(BF16)
    for c0, cw in chunks:
        y = _dot(hb, w_ref[:, c0:c0 + cw])
        for k in range(cw // LANES):
            yg = y[:, k * LANES:(k + 1) * LANES]
            for kind, oc in groups[c0 // LANES + k]:
                if kind == "plain":
                    v = yg
                elif kind == "rms":
                    v = _rms(yg, kvn_ref[...])
                elif kind == "sigmoid":
                    v = _sigmoid(yg)
                else:
                    v = _rope(yg, tab_ref[TAB[kind]], tab_ref[TAB[kind] + 1], WIDTH[kind])
                o_ref[:, :, oc:oc + LANES] = v.reshape(bb, tt, LANES)


def _proj(x, shift, scale, g, w, tabs, kvn, groups, n_out, bb, tt):
    bx, tx, d = x.shape
    nw = w.shape[1]
    chunks = [(c0, min(256, nw - c0)) for c0 in range(0, nw, 256)]
    rows = bb * tt
    tab_map = (lambda i, j: (0, j, 0)) if bb == 1 else (lambda i, j: (0, 0, 0))
    return pl.pallas_call(
        functools.partial(_proj_kernel, chunks=chunks, groups=groups),
        grid=(bx // bb, tx // tt),
        in_specs=[pl.BlockSpec((bb, tt, d), lambda i, j: (i, j, 0)),
                  pl.BlockSpec((bb, 1, d), lambda i, j: (i, 0, 0)),
                  pl.BlockSpec((bb, 1, d), lambda i, j: (i, 0, 0)),
                  pl.BlockSpec((1, 1, d), lambda i, j: (0, 0, 0)),
                  pl.BlockSpec((d, nw), lambda i, j: (0, 0)),
                  pl.BlockSpec((6, rows, LANES), tab_map),
                  pl.BlockSpec((1, LANES), lambda i, j: (0, 0))],
        out_specs=pl.BlockSpec((bb, tt, n_out), lambda i, j: (i, j, 0)),
        out_shape=jax.ShapeDtypeStruct((bx, tx, n_out), F32),
        compiler_params=_params("arbitrary", "arbitrary"), name="proj",
    )(x, shift, scale, g.reshape(1, 1, d), w, tabs, kvn)


def _even_weights(w_in):
    d = w_in.shape[0]
    qa, ka, va, qi, wi, ki, qlat, kvlat = jnp.split(
        w_in, [512, 640, 768, 1280, 1288, 1352, 1608], axis=1)
    z = lambda n: jnp.zeros((d, n), w_in.dtype)
    w = jnp.concatenate([qa, qi, ka, va, ki, z(64), kvlat[:, :B_KVLORA], qlat,
                         kvlat[:, B_KVLORA:], wi, z(LANES - B_ROPE - IDX_HEADS)], axis=1)
    groups = ([[("rope64", EV_QA + LANES * k)] for k in range(4)]
              + [[("rope64", EV_QI + LANES * k)] for k in range(4)]
              + [[("rope64", EV_AKV)], [("plain", EV_AKV + LANES)], [("rope64h", EV_KI)],
                 [("rms", EV_CKV)], [("plain", EV_QLAT)], [("plain", EV_QLAT + LANES)],
                 [("rope32", EV_KR)]])
    return w.astype(BF16), groups


def _odd_weights(w_in):
    d = w_in.shape[0]
    qc, kvc, kvs, kvw, gc, qd, kd, vd = jnp.split(
        w_in, [512, 640, 768, 896, 920, 1432, 1688], axis=1)
    w = jnp.concatenate([qc, qd, kd, vd, kvc, kvs, kvw, gc,
                         jnp.zeros((d, LANES - 3 * C_HEADS), w_in.dtype)], axis=1)
    groups = ([[("plain", OD_QC + LANES * k), ("rope64", OD_QCR + LANES * k)] for k in range(4)]
              + [[("rope64", OD_QD + LANES * k)] for k in range(4)]
              + [[("rope64", OD_DKV)], [("rope64", OD_DKV + LANES)],
                 [("plain", OD_DKV + 2 * LANES)], [("plain", OD_DKV + 3 * LANES)],
                 [("plain", OD_CMP)], [("rope64h", OD_SEL)], [("rope64h", OD_WIN)],
                 [("sigmoid", OD_GATE)]])
    return w.astype(BF16), groups


def _gather_kernel(pt_ref, *refs, plan, n_out):
    g = GATHER_PAGES
    n_pools = len(plan)
    pages = refs[:n_pools * g]
    news = refs[n_pools * g:n_pools * g + n_pools]
    outs = refs[n_pools * g + n_pools:]
    j = pl.program_id(1)
    last = pl.num_programs(1) - 1
    for o_ref in outs:
        o_ref[...] = jnp.zeros(o_ref.shape, o_ref.dtype)

    @pl.when(j < last)
    def _():
        eye = jnp.where(lax.broadcasted_iota(I32, (PAGE_SIZE, PAGE_SIZE), 0)
                        == lax.broadcasted_iota(I32, (PAGE_SIZE, PAGE_SIZE), 1), 1.0, 0.0).astype(BF16)
        for p, (oi, off, width, mode) in enumerate(plan):
            o_ref = outs[oi]
            for k in range(g):
                page = pages[p * g + k]
                r = slice(k * PAGE_SIZE, (k + 1) * PAGE_SIZE)
                if mode == "rows":
                    o_ref[0, r, off:off + width] = page[0, 0].astype(BF16)
                elif mode == "cols":
                    o_ref[0, r, off:off + width] = _dot_nt(eye, page[0, 0].astype(BF16)).astype(BF16)
                else:
                    for q in range(4):
                        o_ref[0, r, off + q * LANES:off + (q + 1) * LANES] = (
                            page[0, 0, pl.ds(q, PAGE_SIZE, stride=4), :].astype(BF16))

    @pl.when(j == last)
    def _():
        for p, (oi, off, width, mode) in enumerate(plan):
            tn = news[p].shape[1]
            outs[oi][0, 0:tn, off:off + width] = news[p][0].astype(BF16)


def _page_view(pool):
    l, n, rows = pool.shape[:3]
    feat = int(np.prod(pool.shape[3:]))
    if pool.shape[-1] >= LANES and pool.ndim == 4:
        return pool, "rows", feat
    if pool.shape[-1] >= LANES:
        assert feat == 4 * LANES
        return pool.reshape(l, n, rows * 4, LANES), "quad", feat
    perm = (0, 1) + tuple(range(3, pool.ndim)) + (2,)
    return pool.transpose(perm).reshape(l, n, feat, rows), "cols", feat


def _gather(page_table, li, groups):
    g = GATHER_PAGES
    b, n_pages = page_table.shape
    assert n_pages % g == 0
    nsteps = n_pages // g + 1
    plan, pool_args, new_args = [], [], []
    for oi, (pools, news, f_tot) in enumerate(groups):
        off = 0
        for pool, nw in zip(pools, news):
            view, mode, width = _page_view(pool)
            plan.append((oi, off, width, mode))
            pool_args.append(view)
            new_args.append(nw)
            off += width
    in_specs, args = [], []
    for p in pool_args:
        for k in range(g):
            in_specs.append(pl.BlockSpec(
                (1, 1) + p.shape[2:],
                lambda i, j, pt, k=k: (li, pt[i, jnp.minimum(j, nsteps - 2) * g + k], 0, 0)))
            args.append(p)
    for nw in new_args:
        in_specs.append(pl.BlockSpec((1,) + nw.shape[1:], lambda i, j, pt: (i, 0, 0)))
        args.append(nw)
    rows = g * PAGE_SIZE
    return pl.pallas_call(
        functools.partial(_gather_kernel, plan=plan, n_out=len(groups)),
        grid_spec=pltpu.PrefetchScalarGridSpec(
            num_scalar_prefetch=1, grid=(b, nsteps), in_specs=in_specs,
            out_specs=[pl.BlockSpec((1, rows, f_tot), lambda i, j, pt: (i, j, 0)) for _, _, f_tot in groups]),
        out_shape=[jax.ShapeDtypeStruct((b, nsteps * rows, f_tot), BF16) for _, _, f_tot in groups],
        compiler_params=_params("arbitrary", "arbitrary"), name="gather",
    )(page_table, *args)


def _online_update(s, mask, v, state):
    m_ref, l_ref, acc_ref = state
    if mask is not None:
        s = jnp.where(mask, s, 2 * NEG)
    m_old = m_ref[...]
    m_new = jnp.maximum(m_old, jnp.max(s, axis=1, keepdims=True))
    p = jnp.exp(s - pltpu.repeat(m_new, s.shape[1] // LANES, axis=1))
    alpha = jnp.exp(m_old - m_new)
    l_ref[...] = alpha * l_ref[...] + jnp.sum(p, axis=1, keepdims=True)
    acc_ref[...] = alpha * acc_ref[...] + _dot(p.astype(BF16), v)
    m_ref[...] = m_new


def _init_state(state):
    m_ref, l_ref, acc_ref = state
    m_ref[...] = jnp.full(m_ref.shape, NEG, F32)
    l_ref[...] = jnp.zeros(l_ref.shape, F32)
    acc_ref[...] = jnp.zeros(acc_ref.shape, F32)


def _finish(state):
    m_ref, l_ref, acc_ref = state
    return acc_ref[...] / jnp.maximum(l_ref[...], 1e-30)


def _state_scratch(rows, n):
    return [pltpu.VMEM((rows, LANES), F32) for _ in range(3 * n)]


def _attend_once(s, mask, v):
    if mask is not None:
        s = jnp.where(mask, s, NEG)
    e = jnp.exp(s - jnp.max(s, axis=1, keepdims=True))
    if mask is not None:
        e = jnp.where(mask, e, 0.0)
    return _dot(e.astype(BF16), v) / jnp.maximum(jnp.sum(e, axis=1, keepdims=True), 1e-30)


def _round_up(x, m):
    return -(-x // m) * m


ONCE_MAX_ROWS = 64


def _once_len(nq, rows, past, tq):
    return _round_up(past + tq, KEY_CHUNK) if nq == 1 and rows <= ONCE_MAX_ROWS else 0


def _row_pos(rows, tq, q0):
    r = lax.broadcasted_iota(I32, (rows, 1), 0)
    return q0 + (r & (tq - 1))


def _sort_key(x):
    b = pltpu.bitcast(x + 0.0, I32)
    return b ^ ((b >> 31) & 0x7FFFFFFF)


def _count(load, nchunks, width, rows, pred):
    lane = lax.broadcasted_iota(I32, (rows, LANES), 1)

    def body(c, acc):
        k = load(c)
        terms = [jnp.where(pred(k[:, j * LANES:(j + 1) * LANES], c * width + j * LANES + lane), 1.0, 0.0)
                 for j in range(width // LANES)]
        while len(terms) > 1:
            terms = [a + b for a, b in zip(terms[0::2], terms[1::2])] + ([terms[-1]] if len(terms) % 2 else [])
        return acc + terms[0]

    zero = jnp.zeros((rows, LANES), F32)
    acc = body(0, zero) if isinstance(nchunks, int) and nchunks == 1 else lax.fori_loop(0, nchunks, body, zero)
    return jnp.sum(acc, axis=1, keepdims=True)


def _topk_threshold(load, nchunks, width, rows, k, idx_bits, j_ref):
    kf = float(k)
    int_min = jnp.int32(-2 ** 31)

    def bit_step(it, t):
        cand = t ^ lax.shift_left(jnp.int32(1), jnp.int32(31) - it)
        cnt = _count(load, nchunks, width, rows, lambda key, idx: key >= cand)
        return jnp.where(cnt >= kf, cand, t)

    t = lax.fori_loop(0, 32, bit_step, jnp.full((rows, LANES), int_min, I32))
    cnt_ge = _count(load, nchunks, width, rows, lambda key, idx: key >= t)
    j_ref[...] = jnp.full((rows, LANES), 2 ** 30, I32)

    @pl.when(jnp.max(cnt_ge) > kf)
    def _():
        need = kf - _count(load, nchunks, width, rows, lambda key, idx: key > t)

        def idx_step(it, p):
            cand = p | lax.shift_left(jnp.int32(1), jnp.int32(idx_bits - 1) - it)
            cnt = _count(load, nchunks, width, rows, lambda key, idx: (key == t) & (idx < cand))
            return jnp.where(cnt < need, cand, p)

        j_ref[...] = lax.fori_loop(0, idx_bits, idx_step, jnp.zeros((rows, LANES), I32))

    return t, j_ref[...]


def _selected(key, idx, t, j):
    n = key.shape[1] // LANES
    if n > 1:
        t, j = pltpu.repeat(t, n, axis=1), pltpu.repeat(j, n, axis=1)
    return jnp.where(key > t, 1.0, jnp.where(key == t, jnp.where(idx <= j, 1.0, 0.0), 0.0))


def _tile_rows(x, n):
    return jnp.concatenate([x] * n, axis=0)


def _index_scores(s, wi, tq):
    acc = jnp.zeros((tq, s.shape[1]), F32)
    for h in range(IDX_HEADS):
        acc = acc + jnp.maximum(s[h * tq:(h + 1) * tq], 0.0) * wi[:, h:h + 1]
    return acc


def _dsa_kernel(qi_ref, wi_ref, qa_ref, ki_ref, kv_ref, o_ref, key_ref, j_ref, *scr,
                tq, past, k_top, idx_bits, n_once):
    ch = KEY_CHUNK
    q0 = past + pl.program_id(1) * tq
    qpos = q0 + lax.broadcasted_iota(I32, (tq, 1), 0)
    qi = qi_ref[0, 0]
    wi = wi_ref[0]
    rep = A_HEADS // A_KVH
    rr = rep * tq

    if n_once:
        lane = lax.broadcasted_iota(I32, (tq, n_once), 1)
        score = jnp.where(lane <= qpos, _index_scores(_dot_nt(qi, ki_ref[0, :n_once, :]), wi, tq), -jnp.inf)
        keys = _sort_key(score)
        t, j = _topk_threshold(lambda c: keys, 1, n_once, tq, k_top, idx_bits, j_ref)
        sel = jnp.where(lane <= qpos, _selected(keys, lane, t, j), 0.0)
        mask = _tile_rows(sel, rep) > 0.0
        kk, vv = kv_ref[0, :n_once, :LANES], kv_ref[0, :n_once, LANES:]
        for g in range(A_KVH):
            o = _attend_once(_dot_nt(qa_ref[0, 0, g * rr:(g + 1) * rr, :], kk), mask, vv)
            o_ref[0, 0, g * rr:(g + 1) * rr, :] = o.astype(BF16)
        return

    nc = (q0 + tq - 1) // ch + 1
    lane = lax.broadcasted_iota(I32, (tq, ch), 1)

    def score_step(c, carry):
        off = pl.multiple_of(c * ch, ch)
        acc = _index_scores(_dot_nt(qi, ki_ref[0, pl.ds(off, ch), :]), wi, tq)
        key_ref[:, pl.ds(off, ch)] = _sort_key(jnp.where(off + lane <= qpos, acc, -jnp.inf))
        return carry

    lax.fori_loop(0, nc, score_step, 0)
    load = lambda c: key_ref[:, pl.ds(pl.multiple_of(c * ch, ch), ch)]
    t, j = _topk_threshold(load, nc, ch, tq, k_top, idx_bits, j_ref)

    states = [scr[3 * g:3 * g + 3] for g in range(A_KVH)]
    for st in states:
        _init_state(st)

    def attend_step(c, carry):
        off = pl.multiple_of(c * ch, ch)
        idx = off + lane
        sel = jnp.where(idx <= qpos, _selected(key_ref[:, pl.ds(off, ch)], idx, t, j), 0.0)
        mask = _tile_rows(sel, rep) > 0.0
        kv = kv_ref[0, pl.ds(off, ch), :]
        kk, vv = kv[:, :LANES], kv[:, LANES:]
        for g in range(A_KVH):
            _online_update(_dot_nt(qa_ref[0, 0, g * rr:(g + 1) * rr, :], kk), mask, vv, states[g])
        return carry

    lax.fori_loop(0, nc, attend_step, 0)
    for g in range(A_KVH):
        o_ref[0, 0, g * rr:(g + 1) * rr, :] = _finish(states[g]).astype(BF16)


def _dsa(qi, wi, qa, ki, kv, tq, past, s_total):
    b, nq, rows, _ = qi.shape
    s_pad = ki.shape[1]
    k_top = min(IDX_TOPK_MAX, s_total // 4)
    rep_rows = rows // A_KVH
    n_once = _once_len(nq, rows, past, tq)
    return pl.pallas_call(
        functools.partial(_dsa_kernel, tq=tq, past=past, k_top=k_top,
                          idx_bits=int(math.ceil(math.log2(s_pad))), n_once=n_once),
        grid=(b, nq),
        in_specs=[pl.BlockSpec((1, 1, rows, IDX_DH), lambda i, j: (i, j, 0, 0)),
                  pl.BlockSpec((1, tq, IDX_HEADS), lambda i, j: (i, j, 0)),
                  pl.BlockSpec((1, 1, rows, LANES), lambda i, j: (i, j, 0, 0)),
                  pl.BlockSpec((1, s_pad, IDX_DH), lambda i, j: (i, 0, 0)),
                  pl.BlockSpec((1, s_pad, 2 * LANES), lambda i, j: (i, 0, 0))],
        out_specs=pl.BlockSpec((1, 1, rows, LANES), lambda i, j: (i, j, 0, 0)),
        out_shape=jax.ShapeDtypeStruct((b, nq, rows, LANES), BF16),
        scratch_shapes=[pltpu.VMEM((tq, LANES if n_once else s_pad), I32), pltpu.VMEM((tq, LANES), I32)]
        + _state_scratch(rep_rows, 0 if n_once else A_KVH),
        compiler_params=_params("arbitrary", "arbitrary"), name="dsa",
    )(qi, wi, qa, ki, kv)


def _mlaq_kernel(ql_ref, qn_ref, wn_ref, wr_ref, wuk_ref, tab_ref, o_ref):
    scale = (B_NOPE + B_ROPE) ** -0.5
    rb = _rms(ql_ref[...], qn_ref[...]).astype(BF16)
    qn_all = _dot(rb, wn_ref[...])
    qr_all = _dot(rb, wr_ref[...])
    for h in range(B_HEADS):
        sl = slice(h * LANES, (h + 1) * LANES)
        q_lat = _dot(qn_all[:, sl].astype(BF16), wuk_ref[h])
        q_rope = _rope(qr_all[:, sl], tab_ref[TAB["rope32"]], tab_ref[TAB["rope32"] + 1], B_ROPE)
        o_ref[0, h, :, 0:LANES] = (q_lat * scale).astype(BF16)
        o_ref[0, h, :, LANES:2 * LANES] = (q_rope * scale).astype(BF16)


def _mlaq(qlat, q_norm, wn, wr, wuk, tabs, tab_blocks):
    n = qlat.shape[0]
    nb = n // QBLK
    tab_map = lambda i: (0, i % tab_blocks, 0)
    return pl.pallas_call(
        _mlaq_kernel,
        grid=(nb,),
        in_specs=[pl.BlockSpec((QBLK, B_QLORA), lambda i: (i, 0)),
                  pl.BlockSpec((1, B_QLORA), lambda i: (0, 0)),
                  pl.BlockSpec((B_QLORA, B_HEADS * LANES), lambda i: (0, 0)),
                  pl.BlockSpec((B_QLORA, B_HEADS * LANES), lambda i: (0, 0)),
                  pl.BlockSpec((B_HEADS, LANES, LANES), lambda i: (0, 0, 0)),
                  pl.BlockSpec((6, QBLK, LANES), tab_map)],
        out_specs=pl.BlockSpec((1, B_HEADS, QBLK, 2 * LANES), lambda i: (i, 0, 0, 0)),
        out_shape=jax.ShapeDtypeStruct((nb, B_HEADS, QBLK, 2 * LANES), BF16),
        compiler_params=_params("arbitrary"), name="mlaq",
    )(qlat, q_norm, wn, wr, wuk, tabs)


def _causal_loops(step, q0, tq, ch=KEY_CHUNK):
    nfull = q0 // ch
    lax.fori_loop(0, nfull, functools.partial(step, False), 0)
    lax.fori_loop(nfull, (q0 + tq - 1) // ch + 1, functools.partial(step, True), 0)


def _mla_kernel(q_ref, kv_ref, o_ref, *state, tq, past, n_once):
    ch = KEY_CHUNK
    rows = q_ref.shape[2]
    q0 = past + pl.program_id(1) * tq
    qpos = _row_pos(rows, tq, q0)
    q = q_ref[0, 0]

    if n_once:
        mask = lax.broadcasted_iota(I32, (1, n_once), 1) <= qpos
        o = _attend_once(_dot_nt(q, kv_ref[0, :n_once, :]), mask, kv_ref[0, :n_once, :LANES])
        o_ref[0, 0] = o.astype(BF16)
        return

    lane = lax.broadcasted_iota(I32, (1, ch), 1)
    _init_state(state)

    def step(masked, c, carry):
        off = pl.multiple_of(c * ch, ch)
        kc = kv_ref[0, pl.ds(off, ch), :]
        _online_update(_dot_nt(q, kc), (off + lane <= qpos) if masked else None, kc[:, :LANES], state)
        return carry

    _causal_loops(step, q0, tq)
    o_ref[0, 0] = _finish(state).astype(BF16)


def _mla(q, kv, tq, past):
    b, nq, rows, f = q.shape
    s_pad = kv.shape[1]
    n_once = _once_len(nq, rows, past, tq)
    return pl.pallas_call(
        functools.partial(_mla_kernel, tq=tq, past=past, n_once=n_once),
        grid=(b, nq),
        in_specs=[pl.BlockSpec((1, 1, rows, f), lambda i, j: (i, j, 0, 0)),
                  pl.BlockSpec((1, s_pad, f), lambda i, j: (i, 0, 0))],
        out_specs=pl.BlockSpec((1, 1, rows, LANES), lambda i, j: (i, j, 0, 0)),
        out_shape=jax.ShapeDtypeStruct((b, nq, rows, LANES), BF16),
        scratch_shapes=_state_scratch(rows, 0 if n_once else 1),
        compiler_params=_params("arbitrary", "arbitrary"), name="mla",
    )(q, kv)


def _mlao_kernel(ol_ref, wuv_ref, o_ref):
    for jp in range(B_HEADS // 2):
        o = _dot(ol_ref[0, 2 * jp], wuv_ref[2 * jp]) + _dot(ol_ref[0, 2 * jp + 1], wuv_ref[2 * jp + 1])
        o_ref[:, jp * LANES:(jp + 1) * LANES] = o.astype(BF16)


def _mlao(olat, wuv):
    nb = olat.shape[0]
    return pl.pallas_call(
        _mlao_kernel,
        grid=(nb,),
        in_specs=[pl.BlockSpec((1, B_HEADS, QBLK, LANES), lambda i: (i, 0, 0, 0)),
                  pl.BlockSpec((B_HEADS, LANES, LANES), lambda i: (0, 0, 0))],
        out_specs=pl.BlockSpec((QBLK, B_HEADS * B_VDH), lambda i: (i, 0)),
        out_shape=jax.ShapeDtypeStruct((nb * QBLK, B_HEADS * B_VDH), BF16),
        compiler_params=_params("arbitrary"), name="mlao",
    )(olat, wuv)


def _compress_kernel(x_ref, pe_ref, phi_ref, o_ref):
    phi = phi_ref[...]
    bias = _dot(pe_ref[...], phi)
    o_ref[0] = _dot(x_ref[0], phi) + bias[0:1]


def _compress(x, pe_rows, phi_big, n_blk):
    b = x.shape[0]
    f = x.shape[2]
    return pl.pallas_call(
        _compress_kernel,
        grid=(b,),
        in_specs=[pl.BlockSpec((1, n_blk, f), lambda i: (i, 0, 0)),
                  pl.BlockSpec((8, f), lambda i: (0, 0)),
                  pl.BlockSpec((f, LANES), lambda i: (0, 0))],
        out_specs=pl.BlockSpec((1, n_blk, LANES), lambda i: (i, 0, 0)),
        out_shape=jax.ShapeDtypeStruct((b, n_blk, LANES), F32),
        compiler_params=_params("arbitrary"), name="compress",
    )(x, pe_rows, phi_big)


def _nsa_kernel(qc_ref, qr_ref, gate_ref, cmp_ref, sel_ref, win_ref, o_ref, j_ref, *state,
                tq, past, win_base, wl, nsel_pad, n_top, n_once):
    ch = KEY_CHUNK
    heads = C_HEADS
    rows = heads * tq
    q0 = past + pl.program_id(1) * tq
    qpos_t = q0 + lax.broadcasted_iota(I32, (tq, 1), 0)
    qpos_r = _row_pos(rows, tq, q0)
    qc = qc_ref[0, 0]
    qr = qr_ref[0, 0]

    kvc = cmp_ref[0]
    nb = kvc.shape[0]
    nbh = nb // 2
    pcol = lax.broadcasted_iota(I32, (1, nb), 1)
    blk = jnp.where(pcol < nbh, 2 * pcol, 2 * (pcol - nbh) + 1)
    cmask = (blk + 1) * CMP_BLK - 1 <= qpos_r
    s_c = jnp.where(cmask, _dot_nt(qc, kvc), NEG)
    e_c = jnp.where(cmask, jnp.exp(s_c - jnp.max(s_c, axis=1, keepdims=True)), 0.0)
    p_c = e_c / jnp.maximum(jnp.sum(e_c, axis=1, keepdims=True), 1e-30)
    o_c = _dot(p_c.astype(BF16), kvc)

    imp = p_c[0:tq]
    for h in range(1, heads):
        imp = imp + p_c[h * tq:(h + 1) * tq]
    imp = imp[:, :nbh] + imp[:, nbh:]
    if nsel_pad > nbh:
        imp = jnp.concatenate([imp, jnp.zeros((tq, nsel_pad - nbh), F32)], axis=1)
    bidx = lax.broadcasted_iota(I32, (tq, nsel_pad), 1)
    forced = (bidx == (qpos_t >> 6)) | (bidx == 0)
    visible = bidx * SEL_BLK <= qpos_t
    imp = jnp.where(visible, jnp.where(forced, FORCE_SCORE, imp), -jnp.inf)
    keys = _sort_key(imp)
    t, j = _topk_threshold(lambda c: keys, 1, nsel_pad, tq, n_top, int(math.log2(nsel_pad)), j_ref)
    bsel = _selected(keys, bidx, t, j).astype(BF16)

    def key_mask(off, width):
        erow = lax.broadcasted_iota(I32, (nsel_pad, width), 0)
        ecol = (off + lax.broadcasted_iota(I32, (nsel_pad, width), 1)) >> 6
        ksel = _dot(bsel, jnp.where(erow == ecol, 1.0, 0.0).astype(BF16))
        visible = off + lax.broadcasted_iota(I32, (tq, width), 1) <= qpos_t
        return _tile_rows(jnp.where(visible, ksel, 0.0), heads) > 0.5

    if n_once:
        kv = sel_ref[0, :n_once, :]
        o_s = _attend_once(_dot_nt(qr, kv), key_mask(0, n_once), kv)
    else:
        _init_state(state)

        def step(c, carry):
            off = pl.multiple_of(c * ch, ch)
            kv = sel_ref[0, pl.ds(off, ch), :]
            _online_update(_dot_nt(qr, kv), key_mask(off, ch), kv, state)
            return carry

        lax.fori_loop(0, (q0 + tq - 1) // ch + 1, step, 0)
        o_s = _finish(state)

    w_rows = win_ref.shape[1]
    start = jnp.clip(q0 - WINDOW - win_base, 0, w_rows - wl)
    start = pl.multiple_of(start, LANES)
    kw = win_ref[0, pl.ds(start, wl), :]
    dist = qpos_t - (win_base + start + lax.broadcasted_iota(I32, (tq, wl), 1))
    in_win = jnp.where(dist >= 0, jnp.where(dist < WINDOW, 1.0, 0.0), 0.0)
    o_w = _attend_once(_dot_nt(qr, kw), _tile_rows(in_win, heads) > 0.5, kw)

    gates = gate_ref[0]
    col = lambda k: jnp.concatenate([gates[:, 3 * h + k:3 * h + k + 1] for h in range(heads)], axis=0)
    o_ref[0, 0] = (col(0) * o_c + col(1) * o_s + col(2) * o_w).astype(BF16)


def _nsa(qc, qr, gates, kvcmp, kvsel, kvwin, tq, past, win_base, s_total):
    b, nq, rows, _ = qc.shape
    nb = kvcmp.shape[1]
    s_pad = kvsel.shape[1]
    w_rows = kvwin.shape[1]
    wl = min(w_rows, -(-(WINDOW + tq) // LANES) * LANES)
    n_sel = -(-s_total // SEL_BLK)
    nsel_pad = -(-n_sel // LANES) * LANES
    n_once = _once_len(nq, rows, past, tq)
    return pl.pallas_call(
        functools.partial(_nsa_kernel, tq=tq, past=past, win_base=win_base, wl=wl,
                          nsel_pad=nsel_pad, n_top=min(N_SEL, n_sel), n_once=n_once),
        grid=(b, nq),
        in_specs=[pl.BlockSpec((1, 1, rows, LANES), lambda i, j: (i, j, 0, 0)),
                  pl.BlockSpec((1, 1, rows, LANES), lambda i, j: (i, j, 0, 0)),
                  pl.BlockSpec((1, tq, LANES), lambda i, j: (i, j, 0)),
                  pl.BlockSpec((1, nb, LANES), lambda i, j: (i, 0, 0)),
                  pl.BlockSpec((1, s_pad, LANES), lambda i, j: (i, 0, 0)),
                  pl.BlockSpec((1, w_rows, LANES), lambda i, j: (i, 0, 0))],
        out_specs=pl.BlockSpec((1, 1, rows, LANES), lambda i, j: (i, j, 0, 0)),
        out_shape=jax.ShapeDtypeStruct((b, nq, rows, LANES), BF16),
        scratch_shapes=[pltpu.VMEM((tq, LANES), I32)] + _state_scratch(rows, 0 if n_once else 1),
        compiler_params=_params("arbitrary", "arbitrary"), name="nsa",
    )(qc, qr, gates, kvcmp, kvsel, kvwin)


def _diff_kernel(q_ref, kv_ref, lam_ref, sub_ref, o_ref, *scr, tq, past, lam_init, n_once):
    ch = DIFF_CHUNK
    rep = D_HEADS // D_KVH
    rows = rep * tq
    q0 = past + pl.program_id(1) * tq
    qpos = _row_pos(rows, tq, q0)
    outs = [None] * (2 * D_KVH)

    if n_once:
        mask = lax.broadcasted_iota(I32, (1, n_once), 1) <= qpos
        for g in range(D_KVH):
            kk = kv_ref[0, :n_once, g * LANES:(g + 1) * LANES]
            vv = kv_ref[0, :n_once, (D_KVH + g) * LANES:(D_KVH + g + 1) * LANES]
            for m in range(2):
                idx = g * 2 + m
                outs[idx] = _attend_once(_dot_nt(q_ref[0, 0, idx * rows:(idx + 1) * rows, :], kk), mask, vv)
    else:
        lane = lax.broadcasted_iota(I32, (1, ch), 1)
        states = [scr[3 * k:3 * k + 3] for k in range(2 * D_KVH)]
        for st in states:
            _init_state(st)

        def step(masked, c, carry):
            off = pl.multiple_of(c * ch, ch)
            mask = (off + lane <= qpos) if masked else None
            kv = kv_ref[0, pl.ds(off, ch), :]
            for g in range(D_KVH):
                kk = kv[:, g * LANES:(g + 1) * LANES]
                vv = kv[:, (D_KVH + g) * LANES:(D_KVH + g + 1) * LANES]
                for m in range(2):
                    idx = g * 2 + m
                    _online_update(_dot_nt(q_ref[0, 0, idx * rows:(idx + 1) * rows, :], kk), mask, vv, states[idx])
            return carry

        _causal_loops(step, q0, tq, ch)
        outs = [_finish(st) for st in states]

    lv = lam_ref[...]
    lam = (jnp.exp(jnp.sum(lv[0:1] * lv[1:2], axis=1, keepdims=True))
           - jnp.exp(jnp.sum(lv[2:3] * lv[3:4], axis=1, keepdims=True)) + lam_init)
    for g in range(D_KVH):
        o = outs[2 * g] - lam * outs[2 * g + 1]
        o = _rms(o, sub_ref[...]) * (1.0 - lam_init)
        for r in range(rep):
            head = g * rep + r
            o_ref[0, :, head * LANES:(head + 1) * LANES] = o[r * tq:(r + 1) * tq]


def _diff(q, kv, lam_vec, subln, tq, past, lam_init):
    b, nq, qrows, _ = q.shape
    s_pad = kv.shape[1]
    rows = (D_HEADS // D_KVH) * tq
    n_once = _once_len(nq, rows, past, tq)
    return pl.pallas_call(
        functools.partial(_diff_kernel, tq=tq, past=past, lam_init=lam_init, n_once=n_once),
        grid=(b, nq),
        in_specs=[pl.BlockSpec((1, 1, qrows, LANES), lambda i, j: (i, j, 0, 0)),
                  pl.BlockSpec((1, s_pad, 4 * LANES), lambda i, j: (i, 0, 0)),
                  pl.BlockSpec((4, D_DH), lambda i, j: (0, 0)),
                  pl.BlockSpec((1, 2 * D_DH), lambda i, j: (0, 0))],
        out_specs=pl.BlockSpec((1, tq, D_HEADS * 2 * D_DH), lambda i, j: (i, j, 0)),
        out_shape=jax.ShapeDtypeStruct((b, nq * tq, D_HEADS * 2 * D_DH), F32),
        scratch_shapes=_state_scratch(rows, 0 if n_once else 2 * D_KVH),
        compiler_params=_params("arbitrary", "arbitrary"), name="diff",
    )(q, kv, lam_vec, subln)


def _route(logits):
    lane = lax.broadcasted_iota(I32, logits.shape, 1)
    lane_f = lane.astype(F32)
    big = 1e9
    is_g = lane < N_GROUPS
    gl = jnp.where(is_g, logits, -jnp.inf)
    gmax = jnp.max(gl, axis=1, keepdims=True)
    grp = jnp.min(jnp.where(gl == gmax, lane_f, big), axis=1, keepdims=True)
    g_w = 1.0 / jnp.sum(jnp.where(is_g, jnp.exp(gl - gmax), 0.0), axis=1, keepdims=True)
    e_grp = ((lane - N_GROUPS) >> 2).astype(F32)
    el = jnp.where(e_grp == grp, logits, -jnp.inf)
    el = jnp.where(lane >= N_GROUPS, jnp.where(lane < N_GROUPS + N_EXPERTS, el, -jnp.inf), -jnp.inf)
    v1 = jnp.max(el, axis=1, keepdims=True)
    i1 = jnp.min(jnp.where(el == v1, lane_f, big), axis=1, keepdims=True)
    el2 = jnp.where(lane_f == i1, -jnp.inf, el)
    v2 = jnp.max(el2, axis=1, keepdims=True)
    i2 = jnp.min(jnp.where(el2 == v2, lane_f, big), axis=1, keepdims=True)
    e21 = jnp.exp(v2 - v1)
    w1 = g_w / (1.0 + e21)
    return jnp.where(lane_f == i1, w1, 0.0) + jnp.where(lane_f == i2, w1 * e21, 0.0)


def _moe_kernel(ma_ref, mb_ref, x_ref, m2_ref, m3_ref, m4_ref, m5_ref, g_ref, woa_ref, wob_ref,
                wrh_ref, wrl_ref, wg_ref, wu_ref, wd_ref, gf_ref, o_ref,
                x1_ref, h_ref, comb_ref, acc_ref, *, final_norm):
    bb, tt, d = x_ref.shape
    tm = bb * tt
    e = pl.program_id(2)

    @pl.when(e == 0)
    def _():
        mix = _dot(ma_ref[...], woa_ref[...]) + _dot(mb_ref[...], wob_ref[...])
        x1 = x_ref[...] + m2_ref[...] * mix.reshape(bb, tt, d)
        h = (_rms(x1, g_ref[...]) * (1.0 + m4_ref[...]) + m3_ref[...]).reshape(tm, d)
        x1_ref[...] = x1.reshape(tm, d)
        h_hi = h.astype(BF16)
        h_lo = (h - h_hi.astype(F32)).astype(BF16)
        h_ref[...] = h_hi
        logits = _dot(h_hi, wrh_ref[...]) + (_dot(h_hi, wrl_ref[...]) + _dot(h_lo, wrh_ref[...]))
        comb_ref[...] = _route(logits)
        acc_ref[...] = jnp.zeros(acc_ref.shape, F32)

    hb = h_ref[...]
    a = _dot(hb, wg_ref[0])
    u = _dot(hb, wu_ref[0])
    lane = lax.broadcasted_iota(I32, (tm, LANES), 1)
    ce = jnp.sum(jnp.where(lane == e + N_GROUPS, comb_ref[...], 0.0), axis=1, keepdims=True)
    act = (a * _sigmoid(a)) * u * ce
    acc_ref[...] += _dot(act.astype(BF16), wd_ref[0])

    @pl.when(e == pl.num_programs(2) - 1)
    def _():
        out = x1_ref[...].reshape(bb, tt, d) + m5_ref[...] * acc_ref[...].reshape(bb, tt, d)
        if final_norm:
            out = _rms(out, gf_ref[...])
        o_ref[...] = out


def _moe(mix_a, mix_b, x, mod, g_ffn, woa, wob, wrh, wrl, wg, wu, wd, g_final, final_norm, bb, tt):
    bx, tx, d = x.shape
    tm = bb * tt
    blk = lambda n: pl.BlockSpec((bb, tt, n), lambda i, j, e: (i, j, 0))
    nj = tx // tt
    flat = lambda a: pl.BlockSpec((tm, a.shape[-1]), lambda i, j, e: (i * nj + j, 0))
    modspec = pl.BlockSpec((bb, 1, d), lambda i, j, e: (i, 0, 0))
    vec = pl.BlockSpec((1, 1, d), lambda i, j, e: (0, 0, 0))
    full = lambda a: pl.BlockSpec(a.shape, lambda i, j, e: (0,) * a.ndim)
    n_e, _, f = wg.shape
    return pl.pallas_call(
        functools.partial(_moe_kernel, final_norm=final_norm),
        grid=(bx // bb, tx // tt, n_e),
        in_specs=[flat(mix_a), flat(mix_b), blk(d), modspec, modspec, modspec, modspec, vec,
                  full(woa), full(wob), full(wrh), full(wrl),
                  pl.BlockSpec((1, d, f), lambda i, j, e: (e, 0, 0)),
                  pl.BlockSpec((1, d, f), lambda i, j, e: (e, 0, 0)),
                  pl.BlockSpec((1, f, d), lambda i, j, e: (e, 0, 0)), vec],
        out_specs=blk(d),
        out_shape=jax.ShapeDtypeStruct((bx, tx, d), F32),
        scratch_shapes=[pltpu.VMEM((tm, d), F32), pltpu.VMEM((tm, d), BF16),
                        pltpu.VMEM((tm, LANES), F32), pltpu.VMEM((tm, d), F32)],
        compiler_params=_params("arbitrary", "arbitrary", "arbitrary"), name="moe",
    )(mix_a.reshape(bx * tx, -1), mix_b.reshape(bx * tx, -1), x, mod[2], mod[3], mod[4], mod[5], g_ffn.reshape(1, 1, d),
      woa, wob, wrh, wrl, wg, wu, wd, g_final.reshape(1, 1, d))


def _head_major(x, nq, tq):
    b, _, h, f = x.shape
    return x.reshape(b, nq, tq, h, f).transpose(0, 1, 3, 2, 4).reshape(b, nq, h * tq, f)


def _token_major(x, tq, h):
    b, nq, _, f = x.shape
    return x.reshape(b, nq, h, tq, f).transpose(0, 1, 3, 2, 4).reshape(b, nq * tq, h, f)


def _lane_pad(x, slot, n_slots=2):
    parts = [jnp.where(jnp.asarray(np.asarray(slot) == s)[:, None], x, 0) for s in range(n_slots)]
    return jnp.concatenate(parts, axis=-1)


def _even_mixer(proj, kv_rows, mla_w, tabs, tab_blocks, tq, past, s_total):
    b, t, _ = proj.shape
    nq = t // tq
    ki, akv, kmla = kv_rows
    qi = _head_major(proj[..., EV_QI:EV_QI + 512].astype(BF16).reshape(b, t, IDX_HEADS, IDX_DH), nq, tq)
    wi = proj[..., EV_WI:EV_WI + IDX_HEADS] * (IDX_HEADS ** -0.5 * IDX_DH ** -0.5)
    qa = (proj[..., EV_QA:EV_QA + 512] * A_DH ** -0.5).astype(BF16).reshape(b, t, A_HEADS, A_DH)
    qa = _head_major(_lane_pad(qa, np.arange(A_HEADS) // (A_HEADS // A_KVH)), nq, tq)
    oa = _dsa(qi, wi, qa, ki, akv, tq, past, s_total)
    oa = _token_major(oa, tq, A_HEADS).reshape(b, t, A_HEADS, A_KVH, A_DH)
    hsel = jnp.asarray(np.arange(A_HEADS) // (A_HEADS // A_KVH))
    oa = jnp.take_along_axis(oa, hsel[None, None, :, None, None], axis=3).reshape(b, t, A_HEADS * A_DH)

    q_norm, wn, wr, wuk, wuv = mla_w
    qp = _mlaq(proj[..., EV_QLAT:EV_QLAT + B_QLORA].reshape(b * t, B_QLORA), q_norm, wn, wr, wuk, tabs, tab_blocks)
    if tq == QBLK:
        qp = qp.reshape(b, nq, B_HEADS * tq, 2 * LANES)
    else:
        per = QBLK // tq
        qp = qp.reshape(b // per, B_HEADS, per, tq, 2 * LANES).transpose(0, 2, 1, 3, 4).reshape(b, 1, B_HEADS * tq, 2 * LANES)
    ol = _mla(qp, kmla, tq, past)
    if tq == QBLK:
        ol = ol.reshape(b * nq, B_HEADS, tq, LANES)
    else:
        per = QBLK // tq
        ol = ol.reshape(b // per, per, B_HEADS, tq, LANES).transpose(0, 2, 1, 3, 4).reshape(b // per, B_HEADS, QBLK, LANES)
    ob = _mlao(ol, wuv).reshape(b, t, B_HEADS * B_VDH)
    return oa, ob


def _odd_mixer(proj, kv_rows, cmp_w, diff_w, tq, past, win_base, s_total, lam_init):
    b, t, _ = proj.shape
    nq = t // tq
    cmp_rows, kvsel, kvwin, dkv = kv_rows
    pe_rows, phi_big = cmp_w
    n_cmp = s_total // CMP_BLK
    kvc = _compress(cmp_rows, pe_rows, phi_big, n_cmp)
    nb = -(-n_cmp // (2 * LANES)) * 2 * LANES
    kvc = jnp.pad(kvc.astype(BF16), ((0, 0), (0, nb - n_cmp), (0, 0)))
    kvc = jnp.concatenate([kvc[:, 0::2], kvc[:, 1::2]], axis=1)
    zeros_slot = np.zeros(C_HEADS, np.int64)
    qc = (proj[..., OD_QC:OD_QC + 512] * C_DH ** -0.5).astype(BF16).reshape(b, t, C_HEADS, C_DH)
    qr = (proj[..., OD_QCR:OD_QCR + 512] * C_DH ** -0.5).astype(BF16).reshape(b, t, C_HEADS, C_DH)
    qc = _head_major(_lane_pad(qc, zeros_slot), nq, tq)
    qr = _head_major(_lane_pad(qr, zeros_slot), nq, tq)
    gates = proj[..., OD_GATE:OD_GATE + LANES]
    oc = _nsa(qc, qr, gates, kvc, kvsel, kvwin, tq, past, win_base, s_total)
    oc = _token_major(oc, tq, C_HEADS)[..., C_DH:].reshape(b, t, C_HEADS * C_DH)

    lam_vec, subln = diff_w
    qd = (proj[..., OD_QD:OD_QD + 512] * D_DH ** -0.5).astype(BF16)
    rep = D_HEADS // D_KVH
    qd = qd.reshape(b, t, D_KVH, rep, 2, D_DH).transpose(0, 1, 2, 4, 3, 5)
    qd = qd.reshape(b, t, D_KVH * 2 * rep, D_DH)
    slot = np.tile(np.repeat(np.arange(2), rep), D_KVH)
    qd = _head_major(_lane_pad(qd, slot), nq, tq)
    od = _diff(qd, dkv, lam_vec, subln, tq, past, lam_init)
    return oc, od.astype(BF16)


def kernel(x_prompt, x_sample, c_prompt, c_sample, cache_a_kv, cache_a_idx, cache_b_latent, cache_b_krope,
           cache_c_cmp, cache_c_sel, state_c_win, cache_d_kv, page_table, w_ada, b_ada, g_mix, g_ffn, w_out,
           w_in_even, b_q_norm, b_kv_norm, w_b_qb, w_b_kvb, w_in_odd, c_phi_k, c_phi_v, c_pe_k, c_pe_v,
           d_lambda, d_subln, moe_w_group, moe_w_expert, moe_w_gate, moe_w_up, moe_w_down, g_final):
    depth = w_ada.shape[0]
    bp, tp, d = x_prompt.shape
    bs, ts, _ = x_sample.shape
    n_pages = page_table.shape[1]
    past = n_pages * PAGE_SIZE
    win_buf = state_c_win.shape[2]
    n_pool = cache_a_kv.shape[1]
    assert tp % KEY_CHUNK == 0 and tp % DIFF_CHUNK == 0 and QBLK % ts == 0 and (bs * ts) % QBLK == 0

    bc = -(-(bp + bs) // 8) * 8
    c_all = jnp.pad(jnp.concatenate([c_prompt, c_sample]), ((0, bc - bp - bs), (0, 0)))
    mod_all = _ada(c_all, w_ada, b_ada).reshape(depth, bc, 6, 1, d)

    pos_p = jnp.arange(tp, dtype=I32)
    pos_s = past + jnp.arange(ts, dtype=I32)
    tabs_p = _all_tables(pos_p)
    bb_s = min(bs, 64)
    tabs_s = jnp.tile(_all_tables(pos_s), (1, max(bb_s, QBLK // ts), 1))

    xp, xs = x_prompt, x_sample
    rows = {k: [] for k in ("akv_p", "akv_s", "idx_p", "idx_s", "lat_p", "lat_s", "kr_p", "kr_s",
                            "cmp_p", "cmp_s", "sel_p", "sel_s", "win_p", "win_s", "dkv_p", "dkv_s")}
    tm_p = 512
    for l in range(depth):
        li = l // 2
        mod_p = [mod_all[l, :bp, k] for k in range(6)]
        mod_s = [mod_all[l, bp:bp + bs, k] for k in range(6)]
        if l % 2 == 0:
            w, groups = _even_weights(w_in_even[li])
            kvn = b_kv_norm[li].reshape(1, LANES)
            pp = _proj(xp, mod_p[0], mod_p[1], g_mix[l], w, tabs_p, kvn, groups, EV_OUT, 1, tm_p)
            ps = _proj(xs, mod_s[0], mod_s[1], g_mix[l], w, tabs_s, kvn, groups, EV_OUT, bb_s, ts)
            wqb = w_b_qb[li]
            pad_h = lambda a: jnp.pad(a, ((0, 0), (0, 0), (0, LANES - a.shape[2]))).reshape(a.shape[0], -1)
            wn = pad_h(wqb[..., :B_NOPE]).astype(BF16)
            wr = pad_h(wqb[..., B_NOPE:]).astype(BF16)
            wkvb = w_b_kvb[li]
            wuk = jnp.pad(wkvb[..., :B_NOPE].transpose(1, 2, 0), ((0, 0), (0, LANES - B_NOPE), (0, 0))).astype(BF16)
            wuv = wkvb[..., B_NOPE:].transpose(1, 0, 2)
            wuv = _lane_pad(wuv.transpose(1, 0, 2), np.arange(B_HEADS) % 2).transpose(1, 0, 2).astype(BF16)
            mla_w = (b_q_norm[li].reshape(1, B_QLORA), wn, wr, wuk, wuv)

            def kv_prompt(p):
                kmla = jnp.concatenate([p[..., EV_CKV:EV_CKV + LANES], p[..., EV_KR:EV_KR + B_ROPE],
                                        jnp.zeros(p.shape[:2] + (LANES - B_ROPE,), F32)], axis=-1)
                return (p[..., EV_KI:EV_KI + IDX_DH].astype(BF16), p[..., EV_AKV:EV_AKV + 256].astype(BF16),
                        kmla.astype(BF16))

            oa_p, ob_p = _even_mixer(pp, kv_prompt(pp), mla_w, tabs_p, tp // QBLK, QBLK, 0, tp)
            new_s = (ps[..., EV_KI:EV_KI + IDX_DH], ps[..., EV_AKV:EV_AKV + 256],
                     ps[..., EV_CKV:EV_CKV + LANES], ps[..., EV_KR:EV_KR + B_ROPE])
            ki_s, akv_s, kmla_s = _gather(page_table, li, [
                ([cache_a_idx], [new_s[0]], IDX_DH), ([cache_a_kv], [new_s[1]], 2 * LANES),
                ([cache_b_latent, cache_b_krope], [new_s[2], new_s[3]], 2 * LANES)])
            oa_s, ob_s = _even_mixer(ps, (ki_s, akv_s, kmla_s), mla_w, tabs_s, 1, ts, past, past + ts)
            for tag, p in (("p", pp), ("s", ps)):
                rows["akv_" + tag].append(p[..., EV_AKV:EV_AKV + 256].reshape(p.shape[:2] + (2, A_KVH, A_DH)))
                rows["idx_" + tag].append(p[..., EV_KI:EV_KI + IDX_DH])
                rows["lat_" + tag].append(p[..., EV_CKV:EV_CKV + B_KVLORA])
                rows["kr_" + tag].append(p[..., EV_KR:EV_KR + B_ROPE])
            mix_p, mix_s = (oa_p, ob_p), (oa_s, ob_s)
        else:
            w, groups = _odd_weights(w_in_odd[li])
            kvn = jnp.ones((1, LANES), F32)
            pp = _proj(xp, mod_p[0], mod_p[1], g_mix[l], w, tabs_p, kvn, groups, OD_OUT, 1, tm_p)
            ps = _proj(xs, mod_s[0], mod_s[1], g_mix[l], w, tabs_s, kvn, groups, OD_OUT, bb_s, ts)
            lam_init = 0.8 - 0.6 * math.exp(-0.3 * l)
            zk = jnp.zeros_like(c_phi_k[li])
            phi_big = jnp.concatenate([jnp.stack([c_phi_k[li], zk], axis=1), jnp.stack([zk, c_phi_v[li]], axis=1)],
                                      axis=-1).reshape(CMP_BLK * 2 * C_DH, 2 * C_DH).astype(BF16)
            pe = jnp.stack([c_pe_k[li], c_pe_v[li]], axis=1).reshape(1, CMP_BLK * 2 * C_DH)
            cmp_w = (jnp.tile(pe, (8, 1)).astype(BF16), phi_big)
            diff_w = (d_lambda[li], d_subln[li].reshape(1, 2 * D_DH))
            blk_rows = lambda a: a.reshape(a.shape[0], a.shape[1] // CMP_BLK, CMP_BLK * 2 * C_DH)
            kv_p = (blk_rows(pp[..., OD_CMP:OD_CMP + LANES].astype(BF16)),
                    pp[..., OD_SEL:OD_SEL + LANES].astype(BF16), pp[..., OD_WIN:OD_WIN + LANES].astype(BF16),
                    pp[..., OD_DKV:OD_DKV + 512].astype(BF16))
            oc_p, od_p = _odd_mixer(pp, kv_p, cmp_w, diff_w, QBLK, 0, 0, tp, lam_init)
            flat = lambda a: a.reshape(a.shape[:3] + (-1,))
            new_cmp, new_sel = ps[..., OD_CMP:OD_CMP + LANES], ps[..., OD_SEL:OD_SEL + LANES]
            new_win, new_dkv = ps[..., OD_WIN:OD_WIN + LANES], ps[..., OD_DKV:OD_DKV + 512]
            cmp_s, sel_s, dkv_s = _gather(page_table, li, [
                ([cache_c_cmp], [new_cmp], LANES), ([cache_c_sel], [new_sel], LANES),
                ([cache_d_kv], [new_dkv], 4 * LANES)])
            win_full = jnp.concatenate([state_c_win[li].reshape(bs, win_buf, LANES), new_win], axis=1)
            w_rows = -(-(win_buf + ts) // LANES) * LANES
            win_s = jnp.pad(win_full, ((0, 0), (0, w_rows - win_buf - ts), (0, 0))).astype(BF16)
            oc_s, od_s = _odd_mixer(ps, (blk_rows(cmp_s), sel_s, win_s, dkv_s), cmp_w, diff_w,
                                    ts, past, past - win_buf, past + ts, lam_init)
            keep_p = min(WINDOW, tp)
            rows["win_p"].append(pp[:, tp - keep_p:, OD_WIN:OD_WIN + LANES].reshape(bp, keep_p, 2, C_DH))
            rows["win_s"].append(win_full[:, ts:].reshape(bs, win_buf, 2, C_DH))
            for tag, p in (("p", pp), ("s", ps)):
                rows["cmp_" + tag].append(p[..., OD_CMP:OD_CMP + LANES].reshape(p.shape[:2] + (2, C_DH)))
                rows["sel_" + tag].append(p[..., OD_SEL:OD_SEL + LANES].reshape(p.shape[:2] + (2, C_DH)))
                rows["dkv_" + tag].append(p[..., OD_DKV:OD_DKV + 512].reshape(p.shape[:2] + (2, D_KVH, 2 * D_DH)))
            mix_p, mix_s = (oc_p, od_p), (oc_s, od_s)

        woa, wob = w_out[l, :512].astype(BF16), w_out[l, 512:].astype(BF16)
        wr_full = jnp.concatenate([moe_w_group[l], moe_w_expert[l],
                                   jnp.zeros((d, LANES - N_GROUPS - N_EXPERTS), F32)], axis=1)
        wrh = wr_full.astype(BF16)
        wrl = (wr_full - wrh.astype(F32)).astype(BF16)
        wg, wu, wd = moe_w_gate[l].astype(BF16), moe_w_up[l].astype(BF16), moe_w_down[l].astype(BF16)
        last = l == depth - 1
        xp = _moe(mix_p[0], mix_p[1], xp, mod_p, g_ffn[l], woa, wob, wrh, wrl, wg, wu, wd, g_final, last,
                  1, min(tp, 1024))
        xs = _moe(mix_s[0], mix_s[1], xs, mod_s, g_ffn[l], woa, wob, wrh, wrl, wg, wu, wd, g_final, last,
                  bs, ts)

    st = lambda k: jnp.stack(rows[k])
    return (xp, xs, st("akv_p"), st("akv_s"), st("idx_p"), st("idx_s"), st("lat_p"), st("lat_s"),
            st("kr_p"), st("kr_s"), st("cmp_p"), st("cmp_s"), st("sel_p"), st("sel_s"),
            st("win_p"), st("win_s"), st("dkv_p"), st("dkv_s"))
```

```python
import functools
import math

import jax
import jax.numpy as jnp
import numpy as np
from jax import lax
from jax.experimental import pallas as pl
from jax.experimental.pallas import tpu as pltpu

F32 = jnp.float32
BF16 = jnp.bfloat16
I32 = jnp.int32

D_MODEL = 1024
PAGE_SIZE = 128
QBLK = 128
ROPE_THETA = 10000.0
EPS = 1e-6
NEG = -1e30
MASKED = 2 * NEG
FORCE_SCORE = 1e9

A_HEADS, A_DH, A_KVH = 8, 64, 2
IDX_HEADS, IDX_DH, IDX_TOPK_MAX = 8, 64, 256
B_HEADS, B_NOPE, B_ROPE, B_VDH, B_QLORA, B_KVLORA = 8, 64, 32, 64, 256, 128
C_HEADS, C_DH, CMP_BLK, SEL_BLK, N_SEL, WINDOW = 8, 64, 32, 64, 16, 512
D_HEADS, D_DH, D_KVH = 4, 64, 2
N_GROUPS, E_PER_GROUP, E_HIDDEN = 4, 4, 256
N_EXPERTS = N_GROUPS * E_PER_GROUP

LANES = 128
KEY_CHUNK = 512
DIFF_CHUNK = 1024
GATHER_PAGES = 8
VMEM_LIMIT = 56 * 1024 * 1024

EV_QA, EV_QI, EV_AKV, EV_KI, EV_CKV, EV_QLAT, EV_KR = 0, 512, 1024, 1280, 1408, 1536, 1792
EV_WI = EV_KR + B_ROPE
EV_OUT = 1920
OD_QC, OD_QCR, OD_QD, OD_DKV, OD_CMP, OD_SEL, OD_WIN, OD_GATE = 0, 512, 1024, 1536, 2048, 2176, 2304, 2432
OD_OUT = 2560


def _dot(a, b):
    return jnp.dot(a, b, preferred_element_type=F32)


def _dot_nt(a, b):
    return lax.dot_general(a, b, (((1,), (1,)), ((), ())), preferred_element_type=F32)


def _params(*sem):
    return pltpu.CompilerParams(dimension_semantics=sem, vmem_limit_bytes=VMEM_LIMIT)


def _rms(x, g):
    return x * lax.rsqrt(jnp.mean(x * x, axis=-1, keepdims=True) + EPS) * g


def _sigmoid(x):
    return 1.0 / (1.0 + jnp.exp(-x))


def _ada_kernel(c_ref, w_ref, b_ref, o_ref):
    c = c_ref[...]
    s = (c * _sigmoid(c)).astype(BF16)
    o_ref[0] = _dot(s, w_ref[0].astype(BF16)) + b_ref[0]


def _ada(c_all, w_ada, b_ada):
    depth, d, n = w_ada.shape
    bc = c_all.shape[0]
    nb = 1536
    return pl.pallas_call(
        _ada_kernel,
        grid=(depth, n // nb),
        in_specs=[pl.BlockSpec((bc, d), lambda l, j: (0, 0)),
                  pl.BlockSpec((1, d, nb), lambda l, j: (l, 0, j)),
                  pl.BlockSpec((1, 1, nb), lambda l, j: (l, 0, j))],
        out_specs=pl.BlockSpec((1, bc, nb), lambda l, j: (l, 0, j)),
        out_shape=jax.ShapeDtypeStruct((depth, bc, n), F32),
        compiler_params=_params("arbitrary", "arbitrary"), name="ada",
    )(c_all, w_ada, b_ada.reshape(depth, 1, n))


def _rope_tables(pos, width, active):
    half = width // 2
    inv = ROPE_THETA ** (-jnp.arange(half, dtype=F32) / half)
    ang = pos.astype(F32)[:, None] * inv[None, :]
    cos, sin = jnp.cos(ang), jnp.sin(ang)
    lane = np.arange(LANES)
    j = lane % width
    cos_l = cos[:, j % half]
    sin_l = sin[:, j % half] * jnp.asarray(np.where(j < half, -1.0, 1.0), F32)[None, :]
    act = jnp.asarray(lane < active)[None, :]
    return jnp.where(act, cos_l, 1.0), jnp.where(act, sin_l, 0.0)


def _all_tables(pos):
    t = []
    for width, active in ((64, 128), (64, 64), (32, 32)):
        t.extend(_rope_tables(pos, width, active))
    return jnp.stack(t)


TAB = {"rope64": 0, "rope64h": 2, "rope32": 4}
WIDTH = {"rope64": 64, "rope64h": 64, "rope32": 32}


def _rope(y, cos, sin, width):
    half = width // 2
    lane = lax.broadcasted_iota(I32, y.shape, 1)
    first = (lane & (width - 1)) < half
    up = pltpu.roll(y, LANES - half, axis=1)
    dn = pltpu.roll(y, half, axis=1)
    return y * cos + jnp.where(first, up, dn) * sin


def _proj_kernel(x_ref, sh_ref, sc_ref, g_ref, w_ref, tab_ref, kvn_ref, o_ref, *, chunks, groups):
    bb, tt, d = x_ref.shape
    h = _rms(x_ref[...], g_ref[...]) * (1.0 + sc_ref[...]) + sh_ref[...]
    hb = h.reshape(bb * tt, d).astype(BF16)
    for c0, cw in chunks:
        y = _dot(hb, w_ref[:, c0:c0 + cw])
        for k in range(cw // LANES):
            yg = y[:, k * LANES:(k + 1) * LANES]
            for kind, oc in groups[c0 // LANES + k]:
                if kind == "plain":
                    v = yg
                elif kind == "rms":
                    v = _rms(yg, kvn_ref[...])
                elif kind == "sigmoid":
                    v = _sigmoid(yg)
                else:
                    v = _rope(yg, tab_ref[TAB[kind]], tab_ref[TAB[kind] + 1], WIDTH[kind])
                o_ref[:, :, oc:oc + LANES] = v.reshape(bb, tt, LANES)


def _proj(x, shift, scale, g, w, tabs, kvn, groups, n_out, bb, tt):
    bx, tx, d = x.shape
    nw = w.shape[1]
    chunks = [(c0, min(256, nw - c0)) for c0 in range(0, nw, 256)]
    rows = bb * tt
    tab_map = (lambda i, j: (0, j, 0)) if bb == 1 else (lambda i, j: (0, 0, 0))
    return pl.pallas_call(
        functools.partial(_proj_kernel, chunks=chunks, groups=groups),
        grid=(bx // bb, tx // tt),
        in_specs=[pl.BlockSpec((bb, tt, d), lambda i, j: (i, j, 0)),
                  pl.BlockSpec((bb, 1, d), lambda i, j: (i, 0, 0)),
                  pl.BlockSpec((bb, 1, d), lambda i, j: (i, 0, 0)),
                  pl.BlockSpec((1, 1, d), lambda i, j: (0, 0, 0)),
                  pl.BlockSpec((d, nw), lambda i, j: (0, 0)),
                  pl.BlockSpec((6, rows, LANES), tab_map),
                  pl.BlockSpec((1, LANES), lambda i, j: (0, 0))],
        out_specs=pl.BlockSpec((bb, tt, n_out), lambda i, j: (i, j, 0)),
        out_shape=jax.ShapeDtypeStruct((bx, tx, n_out), F32),
        compiler_params=_params("arbitrary", "arbitrary"), name="proj",
    )(x, shift, scale, g.reshape(1, 1, d), w, tabs, kvn)


def _even_weights(w_in):
    d = w_in.shape[0]
    qa, ka, va, qi, wi, ki, qlat, kvlat = jnp.split(
        w_in, [512, 640, 768, 1280, 1288, 1352, 1608], axis=1)
    z = lambda n: jnp.zeros((d, n), w_in.dtype)
    w = jnp.concatenate([qa, qi, ka, va, ki, z(64), kvlat[:, :B_KVLORA], qlat,
                         kvlat[:, B_KVLORA:], wi, z(LANES - B_ROPE - IDX_HEADS)], axis=1)
    groups = ([[("rope64", EV_QA + LANES * k)] for k in range(4)]
              + [[("rope64", EV_QI + LANES * k)] for k in range(4)]
              + [[("rope64", EV_AKV)], [("plain", EV_AKV + LANES)], [("rope64h", EV_KI)],
                 [("rms", EV_CKV)], [("plain", EV_QLAT)], [("plain", EV_QLAT + LANES)],
                 [("rope32", EV_KR)]])
    return w.astype(BF16), groups


def _odd_weights(w_in):
    d = w_in.shape[0]
    qc, kvc, kvs, kvw, gc, qd, kd, vd = jnp.split(
        w_in, [512, 640, 768, 896, 920, 1432, 1688], axis=1)
    w = jnp.concatenate([qc, qd, kd, vd, kvc, kvs, kvw, gc,
                         jnp.zeros((d, LANES - 3 * C_HEADS), w_in.dtype)], axis=1)
    groups = ([[("plain", OD_QC + LANES * k), ("rope64", OD_QCR + LANES * k)] for k in range(4)]
              + [[("rope64", OD_QD + LANES * k)] for k in range(4)]
              + [[("rope64", OD_DKV)], [("rope64", OD_DKV + LANES)],
                 [("plain", OD_DKV + 2 * LANES)], [("plain", OD_DKV + 3 * LANES)],
                 [("plain", OD_CMP)], [("rope64h", OD_SEL)], [("rope64h", OD_WIN)],
                 [("sigmoid", OD_GATE)]])
    return w.astype(BF16), groups


def _gather_kernel(pt_ref, *refs, plan, n_out):
    g = GATHER_PAGES
    n_pools = len(plan)
    pages = refs[:n_pools * g]
    news = refs[n_pools * g:n_pools * g + n_pools]
    outs = refs[n_pools * g + n_pools:]
    j = pl.program_id(1)
    last = pl.num_programs(1) - 1
    covered = [sum(w for oi, _, w, _ in plan if oi == k) for k in range(n_out)]

    @pl.when(j < last)
    def _():
        for k, o_ref in enumerate(outs):
            if covered[k] < o_ref.shape[2]:
                o_ref[0, :, covered[k]:] = jnp.zeros((o_ref.shape[1], o_ref.shape[2] - covered[k]), o_ref.dtype)
        eye = jnp.where(lax.broadcasted_iota(I32, (PAGE_SIZE, PAGE_SIZE), 0)
                        == lax.broadcasted_iota(I32, (PAGE_SIZE, PAGE_SIZE), 1), 1.0, 0.0).astype(BF16)
        for p, (oi, off, width, mode) in enumerate(plan):
            o_ref = outs[oi]
            for k in range(g):
                page = pages[p * g + k]
                r = slice(k * PAGE_SIZE, (k + 1) * PAGE_SIZE)
                if mode == "rows":
                    o_ref[0, r, off:off + width] = page[0, 0].astype(BF16)
                elif mode == "cols":
                    o_ref[0, r, off:off + width] = _dot_nt(eye, page[0, 0].astype(BF16)).astype(BF16)
                else:
                    for q in range(4):
                        o_ref[0, r, off + q * LANES:off + (q + 1) * LANES] = (
                            page[0, 0, pl.ds(q, PAGE_SIZE, stride=4), :].astype(BF16))

    @pl.when(j == last)
    def _():
        for o_ref in outs:
            o_ref[...] = jnp.zeros(o_ref.shape, o_ref.dtype)
        for p, (oi, off, width, mode) in enumerate(plan):
            tn = news[p].shape[1]
            outs[oi][0, 0:tn, off:off + width] = news[p][0].astype(BF16)


def _page_view(pool):
    l, n, rows = pool.shape[:3]
    feat = int(np.prod(pool.shape[3:]))
    if pool.shape[-1] >= LANES and pool.ndim == 4:
        return pool, "rows", feat
    if pool.shape[-1] >= LANES:
        assert feat == 4 * LANES
        return pool.reshape(l, n, rows * 4, LANES), "quad", feat
    perm = (0, 1) + tuple(range(3, pool.ndim)) + (2,)
    return pool.transpose(perm).reshape(l, n, feat, rows), "cols", feat


def _gather(page_table, li, groups):
    g = GATHER_PAGES
    b, n_pages = page_table.shape
    assert n_pages % g == 0
    nsteps = n_pages // g + 1
    plan, pool_args, new_args = [], [], []
    for oi, (pools, news, f_tot) in enumerate(groups):
        off = 0
        for pool, nw in zip(pools, news):
            view, mode, width = _page_view(pool)
            plan.append((oi, off, width, mode))
            pool_args.append(view)
            new_args.append(nw)
            off += width
    in_specs, args = [], []
    for p in pool_args:
        for k in range(g):
            in_specs.append(pl.BlockSpec(
                (1, 1) + p.shape[2:],
                lambda i, j, pt, k=k: (li, pt[i, jnp.minimum(j, nsteps - 2) * g + k], 0, 0)))
            args.append(p)
    for nw in new_args:
        in_specs.append(pl.BlockSpec((1,) + nw.shape[1:], lambda i, j, pt: (i, 0, 0)))
        args.append(nw)
    rows = g * PAGE_SIZE
    return pl.pallas_call(
        functools.partial(_gather_kernel, plan=plan, n_out=len(groups)),
        grid_spec=pltpu.PrefetchScalarGridSpec(
            num_scalar_prefetch=1, grid=(b, nsteps), in_specs=in_specs,
            out_specs=[pl.BlockSpec((1, rows, f_tot), lambda i, j, pt: (i, j, 0)) for _, _, f_tot in groups]),
        out_shape=[jax.ShapeDtypeStruct((b, nsteps * rows, f_tot), BF16) for _, _, f_tot in groups],
        compiler_params=_params("arbitrary", "arbitrary"), name="gather",
    )(page_table, *args)


def _online_update(s, mask, v, state):
    m_ref, l_ref, acc_ref = state
    if mask is not None:
        s = jnp.where(mask, s, MASKED) if mask.dtype == jnp.bool_ else s + mask
    m_old = m_ref[...]
    m_new = jnp.maximum(m_old, jnp.max(s, axis=1, keepdims=True))
    p = jnp.exp(s - jnp.tile(m_new, (1, s.shape[1] // LANES)))
    alpha = jnp.exp(m_old - m_new)
    l_ref[...] = alpha * l_ref[...] + jnp.sum(p, axis=1, keepdims=True)
    acc_ref[...] = alpha * acc_ref[...] + _dot(p.astype(BF16), v)
    m_ref[...] = m_new


def _init_state(state):
    m_ref, l_ref, acc_ref = state
    m_ref[...] = jnp.full(m_ref.shape, NEG, F32)
    l_ref[...] = jnp.zeros(l_ref.shape, F32)
    acc_ref[...] = jnp.zeros(acc_ref.shape, F32)


def _finish(state):
    m_ref, l_ref, acc_ref = state
    return acc_ref[...] / jnp.maximum(l_ref[...], 1e-30)


def _state_scratch(rows, n):
    return [pltpu.VMEM((rows, LANES), F32) for _ in range(3 * n)]


def _attend_once(s, mask, v):
    if mask is not None:
        s = jnp.where(mask, s, NEG)
    e = jnp.exp(s - jnp.max(s, axis=1, keepdims=True))
    if mask is not None:
        e = jnp.where(mask, e, 0.0)
    return _dot(e.astype(BF16), v) / jnp.maximum(jnp.sum(e, axis=1, keepdims=True), 1e-30)


def _round_up(x, m):
    return -(-x // m) * m


ONCE_MAX_ROWS = 64


def _once_len(nq, rows, past, tq):
    return _round_up(past + tq, KEY_CHUNK) if nq == 1 and rows <= ONCE_MAX_ROWS else 0


def _row_pos(rows, tq, q0):
    r = lax.broadcasted_iota(I32, (rows, 1), 0)
    return q0 + (r & (tq - 1))


def _sort_key(x):
    b = pltpu.bitcast(x + 0.0, I32)
    return b ^ ((b >> 31) & 0x7FFFFFFF)


def _count(load, nchunks, width, rows, pred):
    lane = lax.broadcasted_iota(I32, (rows, LANES), 1)

    def body(c, acc):
        k = load(c)
        return acc + _tree_sum([
            jnp.where(pred(k[:, j * LANES:(j + 1) * LANES], c * width + j * LANES + lane), 1.0, 0.0)
            for j in range(width // LANES)])

    zero = jnp.zeros((rows, LANES), F32)
    acc = body(0, zero) if isinstance(nchunks, int) and nchunks == 1 else lax.fori_loop(0, nchunks, body, zero)
    return jnp.sum(acc, axis=1, keepdims=True)


def _tree_sum(terms):
    while len(terms) > 1:
        terms = [a + b for a, b in zip(terms[0::2], terms[1::2])] + ([terms[-1]] if len(terms) % 2 else [])
    return terms[0]


def _count_ge(load, nchunks, width, rows, cands):
    def body(c, accs):
        k = load(c)
        tiles = [k[:, j * LANES:(j + 1) * LANES] for j in range(width // LANES)]
        return tuple(acc + _tree_sum([jnp.where(kt >= cand, 1.0, 0.0) for kt in tiles])
                     for cand, acc in zip(cands, accs))

    zero = tuple(jnp.zeros((rows, LANES), F32) for _ in cands)
    accs = body(0, zero) if isinstance(nchunks, int) and nchunks == 1 else lax.fori_loop(0, nchunks, body, zero)
    return [jnp.sum(a, axis=1, keepdims=True) for a in accs]


def _topk_threshold(load, nchunks, width, rows, k, idx_bits, j_ref):
    kf = float(k)
    int_min = jnp.int32(-2 ** 31)

    def pair_step(it, t):
        lo = lax.shift_left(jnp.int32(1), jnp.int32(30) - 2 * it)
        hi = lax.shift_left(lo, 1)
        c1, c2, c3 = t ^ lo, t ^ hi, t ^ (lo | hi)
        n1, n2, n3 = _count_ge(load, nchunks, width, rows, (c1, c2, c3))
        return jnp.where(n3 >= kf, c3, jnp.where(n2 >= kf, c2, jnp.where(n1 >= kf, c1, t)))

    def bit_step(it, t):
        cand = t ^ lax.shift_left(jnp.int32(1), jnp.int32(31) - it)
        return jnp.where(_count_ge(load, nchunks, width, rows, (cand,))[0] >= kf, cand, t)

    t0 = jnp.full((rows, LANES), int_min, I32)
    if isinstance(nchunks, int) and nchunks == 1:
        t = lax.fori_loop(0, 16, pair_step, t0)
    else:
        t = lax.fori_loop(0, 32, bit_step, t0)
    cnt_ge = _count(load, nchunks, width, rows, lambda key, idx: key >= t)
    j_ref[...] = jnp.full((rows, LANES), 2 ** 30, I32)

    @pl.when(jnp.max(cnt_ge) > kf)
    def _():
        need = kf - _count(load, nchunks, width, rows, lambda key, idx: key > t)

        def idx_step(it, p):
            cand = p | lax.shift_left(jnp.int32(1), jnp.int32(idx_bits - 1) - it)
            cnt = _count(load, nchunks, width, rows, lambda key, idx: (key == t) & (idx < cand))
            return jnp.where(cnt < need, cand, p)

        j_ref[...] = lax.fori_loop(0, idx_bits, idx_step, jnp.zeros((rows, LANES), I32))

    return t, j_ref[...]


def _selected(key, idx, t, j):
    return _select_values(key, idx, t, j, 1.0, 0.0)


def _select_values(key, idx, t, j, yes, no):
    n = key.shape[1] // LANES
    if n > 1:
        t, j = jnp.tile(t, (1, n)), jnp.tile(j, (1, n))
    return jnp.where(key > t, yes, jnp.where(key == t, jnp.where(idx <= j, yes, no), no))


def _tile_rows(x, n):
    return jnp.concatenate([x] * n, axis=0)


def _index_scores(s, wi, tq):
    acc = jnp.zeros((tq, s.shape[1]), F32)
    for h in range(IDX_HEADS):
        acc = acc + jnp.maximum(s[h * tq:(h + 1) * tq], 0.0) * wi[:, h:h + 1]
    return acc


def _dsa_kernel(qi_ref, wi_ref, qa_ref, ki_ref, kv_ref, o_ref, key_ref, j_ref, *scr,
                tq, past, k_top, idx_bits, n_once):
    ch = KEY_CHUNK
    q0 = past + pl.program_id(1) * tq
    qpos = q0 + lax.broadcasted_iota(I32, (tq, 1), 0)
    qi = qi_ref[0, 0]
    wi = wi_ref[0]
    rep = A_HEADS // A_KVH
    rr = rep * tq

    if n_once:
        lane = lax.broadcasted_iota(I32, (tq, n_once), 1)
        score = jnp.where(lane <= qpos, _index_scores(_dot_nt(qi, ki_ref[0, :n_once, :]), wi, tq), -jnp.inf)
        keys = _sort_key(score)
        t, j = _topk_threshold(lambda c: keys, 1, n_once, tq, k_top, idx_bits, j_ref)
        sel = jnp.where(lane <= qpos, _selected(keys, lane, t, j), 0.0)
        mask = _tile_rows(sel, rep) > 0.0
        kk, vv = kv_ref[0, :n_once, :LANES], kv_ref[0, :n_once, LANES:]
        for g in range(A_KVH):
            o = _attend_once(_dot_nt(qa_ref[0, 0, g * rr:(g + 1) * rr, :], kk), mask, vv)
            o_ref[0, 0, g * rr:(g + 1) * rr, :] = o.astype(BF16)
        return

    nc = (q0 + tq - 1) // ch + 1
    lane = lax.broadcasted_iota(I32, (tq, ch), 1)

    def score_step(c, carry):
        off = pl.multiple_of(c * ch, ch)
        acc = _index_scores(_dot_nt(qi, ki_ref[0, pl.ds(off, ch), :]), wi, tq)
        key_ref[:, pl.ds(off, ch)] = _sort_key(jnp.where(off + lane <= qpos, acc, -jnp.inf))
        return carry

    lax.fori_loop(0, nc, score_step, 0)
    load = lambda c: key_ref[:, pl.ds(pl.multiple_of(c * ch, ch), ch)]
    t, j = _topk_threshold(load, nc, ch, tq, k_top, idx_bits, j_ref)

    states = [scr[3 * g:3 * g + 3] for g in range(A_KVH)]
    for st in states:
        _init_state(st)

    def attend_step(c, carry):
        off = pl.multiple_of(c * ch, ch)
        idx = off + lane
        bias = jnp.where(idx <= qpos, _select_values(key_ref[:, pl.ds(off, ch)], idx, t, j, 0.0, MASKED), MASKED)
        mask = _tile_rows(bias, rep)
        kv = kv_ref[0, pl.ds(off, ch), :]
        kk, vv = kv[:, :LANES], kv[:, LANES:]
        for g in range(A_KVH):
            _online_update(_dot_nt(qa_ref[0, 0, g * rr:(g + 1) * rr, :], kk), mask, vv, states[g])
        return carry

    lax.fori_loop(0, nc, attend_step, 0)
    for g in range(A_KVH):
        o_ref[0, 0, g * rr:(g + 1) * rr, :] = _finish(states[g]).astype(BF16)


def _dsa(qi, wi, qa, ki, kv, tq, past, s_total):
    b, nq, rows, _ = qi.shape
    s_pad = ki.shape[1]
    k_top = min(IDX_TOPK_MAX, s_total // 4)
    rep_rows = rows // A_KVH
    n_once = _once_len(nq, rows, past, tq)
    return pl.pallas_call(
        functools.partial(_dsa_kernel, tq=tq, past=past, k_top=k_top,
                          idx_bits=int(math.ceil(math.log2(s_pad))), n_once=n_once),
        grid=(b, nq),
        in_specs=[pl.BlockSpec((1, 1, rows, IDX_DH), lambda i, j: (i, j, 0, 0)),
                  pl.BlockSpec((1, tq, IDX_HEADS), lambda i, j: (i, j, 0)),
                  pl.BlockSpec((1, 1, rows, LANES), lambda i, j: (i, j, 0, 0)),
                  pl.BlockSpec((1, s_pad, IDX_DH), lambda i, j: (i, 0, 0)),
                  pl.BlockSpec((1, s_pad, 2 * LANES), lambda i, j: (i, 0, 0))],
        out_specs=pl.BlockSpec((1, 1, rows, LANES), lambda i, j: (i, j, 0, 0)),
        out_shape=jax.ShapeDtypeStruct((b, nq, rows, LANES), BF16),
        scratch_shapes=[pltpu.VMEM((tq, LANES if n_once else s_pad), I32), pltpu.VMEM((tq, LANES), I32)]
        + _state_scratch(rep_rows, 0 if n_once else A_KVH),
        compiler_params=_params("arbitrary", "arbitrary"), name="dsa",
    )(qi, wi, qa, ki, kv)


def _mlaq_kernel(ql_ref, qn_ref, wn_ref, wr_ref, wuk_ref, tab_ref, o_ref):
    scale = (B_NOPE + B_ROPE) ** -0.5
    rb = _rms(ql_ref[...], qn_ref[...]).astype(BF16)
    qn_all = _dot(rb, wn_ref[...])
    qr_all = _dot(rb, wr_ref[...])
    for h in range(B_HEADS):
        sl = slice(h * LANES, (h + 1) * LANES)
        q_lat = _dot(qn_all[:, sl].astype(BF16), wuk_ref[h])
        q_rope = _rope(qr_all[:, sl], tab_ref[TAB["rope32"]], tab_ref[TAB["rope32"] + 1], B_ROPE)
        o_ref[0, h, :, 0:LANES] = (q_lat * scale).astype(BF16)
        o_ref[0, h, :, LANES:2 * LANES] = (q_rope * scale).astype(BF16)


def _mlaq(qlat, q_norm, wn, wr, wuk, tabs, tab_blocks):
    n = qlat.shape[0]
    nb = n // QBLK
    tab_map = lambda i: (0, i % tab_blocks, 0)
    return pl.pallas_call(
        _mlaq_kernel,
        grid=(nb,),
        in_specs=[pl.BlockSpec((QBLK, B_QLORA), lambda i: (i, 0)),
                  pl.BlockSpec((1, B_QLORA), lambda i: (0, 0)),
                  pl.BlockSpec((B_QLORA, B_HEADS * LANES), lambda i: (0, 0)),
                  pl.BlockSpec((B_QLORA, B_HEADS * LANES), lambda i: (0, 0)),
                  pl.BlockSpec((B_HEADS, LANES, LANES), lambda i: (0, 0, 0)),
                  pl.BlockSpec((6, QBLK, LANES), tab_map)],
        out_specs=pl.BlockSpec((1, B_HEADS, QBLK, 2 * LANES), lambda i: (i, 0, 0, 0)),
        out_shape=jax.ShapeDtypeStruct((nb, B_HEADS, QBLK, 2 * LANES), BF16),
        compiler_params=_params("arbitrary"), name="mlaq",
    )(qlat, q_norm, wn, wr, wuk, tabs)


def _causal_loops(step, q0, tq, ch=KEY_CHUNK):
    nfull = q0 // ch
    lax.fori_loop(0, nfull, functools.partial(step, False), 0)
    lax.fori_loop(nfull, (q0 + tq - 1) // ch + 1, functools.partial(step, True), 0)


def _mla_kernel(q_ref, kv_ref, o_ref, *state, tq, past, n_once):
    ch = KEY_CHUNK
    rows = q_ref.shape[2]
    q0 = past + pl.program_id(1) * tq
    qpos = _row_pos(rows, tq, q0)
    q = q_ref[0, 0]

    if n_once:
        mask = lax.broadcasted_iota(I32, (1, n_once), 1) <= qpos
        o = _attend_once(_dot_nt(q, kv_ref[0, :n_once, :]), mask, kv_ref[0, :n_once, :LANES])
        o_ref[0, 0] = o.astype(BF16)
        return

    lane = lax.broadcasted_iota(I32, (1, ch), 1)
    _init_state(state)

    def step(masked, c, carry):
        off = pl.multiple_of(c * ch, ch)
        kc = kv_ref[0, pl.ds(off, ch), :]
        _online_update(_dot_nt(q, kc), (off + lane <= qpos) if masked else None, kc[:, :LANES], state)
        return carry

    _causal_loops(step, q0, tq)
    o_ref[0, 0] = _finish(state).astype(BF16)


def _mla(q, kv, tq, past):
    b, nq, rows, f = q.shape
    s_pad = kv.shape[1]
    n_once = _once_len(nq, rows, past, tq)
    return pl.pallas_call(
        functools.partial(_mla_kernel, tq=tq, past=past, n_once=n_once),
        grid=(b, nq),
        in_specs=[pl.BlockSpec((1, 1, rows, f), lambda i, j: (i, j, 0, 0)),
                  pl.BlockSpec((1, s_pad, f), lambda i, j: (i, 0, 0))],
        out_specs=pl.BlockSpec((1, 1, rows, LANES), lambda i, j: (i, j, 0, 0)),
        out_shape=jax.ShapeDtypeStruct((b, nq, rows, LANES), BF16),
        scratch_shapes=_state_scratch(rows, 0 if n_once else 1),
        compiler_params=_params("arbitrary", "arbitrary"), name="mla",
    )(q, kv)


def _mlao_kernel(ol_ref, wuv_ref, o_ref):
    for jp in range(B_HEADS // 2):
        o = _dot(ol_ref[0, 2 * jp], wuv_ref[2 * jp]) + _dot(ol_ref[0, 2 * jp + 1], wuv_ref[2 * jp + 1])
        o_ref[:, jp * LANES:(jp + 1) * LANES] = o.astype(BF16)


def _mlao(olat, wuv):
    nb = olat.shape[0]
    return pl.pallas_call(
        _mlao_kernel,
        grid=(nb,),
        in_specs=[pl.BlockSpec((1, B_HEADS, QBLK, LANES), lambda i: (i, 0, 0, 0)),
                  pl.BlockSpec((B_HEADS, LANES, LANES), lambda i: (0, 0, 0))],
        out_specs=pl.BlockSpec((QBLK, B_HEADS * B_VDH), lambda i: (i, 0)),
        out_shape=jax.ShapeDtypeStruct((nb * QBLK, B_HEADS * B_VDH), BF16),
        compiler_params=_params("arbitrary"), name="mlao",
    )(olat, wuv)


def _compress_kernel(x_ref, pe_ref, phi_ref, o_ref):
    phi = phi_ref[...]
    bias = _dot(pe_ref[...], phi)
    o_ref[0] = _dot(x_ref[0], phi) + bias[0:1]


def _compress(x, pe_rows, phi_big, n_blk):
    b = x.shape[0]
    f = x.shape[2]
    return pl.pallas_call(
        _compress_kernel,
        grid=(b,),
        in_specs=[pl.BlockSpec((1, n_blk, f), lambda i: (i, 0, 0)),
                  pl.BlockSpec((8, f), lambda i: (0, 0)),
                  pl.BlockSpec((f, LANES), lambda i: (0, 0))],
        out_specs=pl.BlockSpec((1, n_blk, LANES), lambda i: (i, 0, 0)),
        out_shape=jax.ShapeDtypeStruct((b, n_blk, LANES), F32),
        compiler_params=_params("arbitrary"), name="compress",
    )(x, pe_rows, phi_big)


def _nsa_kernel(qc_ref, qr_ref, gate_ref, cmp_ref, sel_ref, win_ref, o_ref, j_ref, *state,
                tq, past, win_base, wl, nsel_pad, n_top, n_once):
    ch = KEY_CHUNK
    heads = C_HEADS
    rows = heads * tq
    q0 = past + pl.program_id(1) * tq
    qpos_t = q0 + lax.broadcasted_iota(I32, (tq, 1), 0)
    qpos_r = _row_pos(rows, tq, q0)
    qc = qc_ref[0, 0]
    qr = qr_ref[0, 0]

    kvc = cmp_ref[0]
    nb = kvc.shape[0]
    nbh = nb // 2
    pcol = lax.broadcasted_iota(I32, (1, nb), 1)
    blk = jnp.where(pcol < nbh, 2 * pcol, 2 * (pcol - nbh) + 1)
    cmask = (blk + 1) * CMP_BLK - 1 <= qpos_r
    s_c = jnp.where(cmask, _dot_nt(qc, kvc), NEG)
    e_c = jnp.where(cmask, jnp.exp(s_c - jnp.max(s_c, axis=1, keepdims=True)), 0.0)
    p_c = e_c / jnp.maximum(jnp.sum(e_c, axis=1, keepdims=True), 1e-30)
    o_c = _dot(p_c.astype(BF16), kvc)

    imp = p_c[0:tq]
    for h in range(1, heads):
        imp = imp + p_c[h * tq:(h + 1) * tq]
    imp = imp[:, :nbh] + imp[:, nbh:]
    if nsel_pad > nbh:
        imp = jnp.concatenate([imp, jnp.zeros((tq, nsel_pad - nbh), F32)], axis=1)
    bidx = lax.broadcasted_iota(I32, (tq, nsel_pad), 1)
    forced = (bidx == (qpos_t >> 6)) | (bidx == 0)
    visible = bidx * SEL_BLK <= qpos_t
    imp = jnp.where(visible, jnp.where(forced, FORCE_SCORE, imp), -jnp.inf)
    keys = _sort_key(imp)
    t, j = _topk_threshold(lambda c: keys, 1, nsel_pad, tq, n_top, int(math.log2(nsel_pad)), j_ref)
    bsel = _selected(keys, bidx, t, j).astype(BF16)

    def key_mask(off, width):
        erow = lax.broadcasted_iota(I32, (nsel_pad, width), 0)
        ecol = (off + lax.broadcasted_iota(I32, (nsel_pad, width), 1)) >> 6
        ksel = _dot(bsel, jnp.where(erow == ecol, 1.0, 0.0).astype(BF16))
        visible = off + lax.broadcasted_iota(I32, (tq, width), 1) <= qpos_t
        return _tile_rows(jnp.where(visible, jnp.where(ksel > 0.5, 0.0, MASKED), MASKED), heads)

    if n_once:
        kv = sel_ref[0, :n_once, :]
        o_s = _attend_once(_dot_nt(qr, kv), key_mask(0, n_once) > NEG, kv)
    else:
        _init_state(state)

        def step(c, carry):
            off = pl.multiple_of(c * ch, ch)
            kv = sel_ref[0, pl.ds(off, ch), :]
            _online_update(_dot_nt(qr, kv), key_mask(off, ch), kv, state)
            return carry

        lax.fori_loop(0, (q0 + tq - 1) // ch + 1, step, 0)
        o_s = _finish(state)

    w_rows = win_ref.shape[1]
    start = jnp.clip(q0 - WINDOW - win_base, 0, w_rows - wl)
    start = pl.multiple_of(start, LANES)
    kw = win_ref[0, pl.ds(start, wl), :]
    dist = qpos_t - (win_base + start + lax.broadcasted_iota(I32, (tq, wl), 1))
    in_win = jnp.where(dist >= 0, jnp.where(dist < WINDOW, 1.0, 0.0), 0.0)
    o_w = _attend_once(_dot_nt(qr, kw), _tile_rows(in_win, heads) > 0.5, kw)

    gates = gate_ref[0]
    col = lambda k: jnp.concatenate([gates[:, 3 * h + k:3 * h + k + 1] for h in range(heads)], axis=0)
    o_ref[0, 0] = (col(0) * o_c + col(1) * o_s + col(2) * o_w).astype(BF16)


def _nsa(qc, qr, gates, kvcmp, kvsel, kvwin, tq, past, win_base, s_total):
    b, nq, rows, _ = qc.shape
    nb = kvcmp.shape[1]
    s_pad = kvsel.shape[1]
    w_rows = kvwin.shape[1]
    wl = min(w_rows, -(-(WINDOW + tq) // LANES) * LANES)
    n_sel = -(-s_total // SEL_BLK)
    nsel_pad = -(-n_sel // LANES) * LANES
    n_once = _once_len(nq, rows, past, tq)
    return pl.pallas_call(
        functools.partial(_nsa_kernel, tq=tq, past=past, win_base=win_base, wl=wl,
                          nsel_pad=nsel_pad, n_top=min(N_SEL, n_sel), n_once=n_once),
        grid=(b, nq),
        in_specs=[pl.BlockSpec((1, 1, rows, LANES), lambda i, j: (i, j, 0, 0)),
                  pl.BlockSpec((1, 1, rows, LANES), lambda i, j: (i, j, 0, 0)),
                  pl.BlockSpec((1, tq, LANES), lambda i, j: (i, j, 0)),
                  pl.BlockSpec((1, nb, LANES), lambda i, j: (i, 0, 0)),
                  pl.BlockSpec((1, s_pad, LANES), lambda i, j: (i, 0, 0)),
                  pl.BlockSpec((1, w_rows, LANES), lambda i, j: (i, 0, 0))],
        out_specs=pl.BlockSpec((1, 1, rows, LANES), lambda i, j: (i, j, 0, 0)),
        out_shape=jax.ShapeDtypeStruct((b, nq, rows, LANES), BF16),
        scratch_shapes=[pltpu.VMEM((tq, LANES), I32)] + _state_scratch(rows, 0 if n_once else 1),
        compiler_params=_params("arbitrary", "arbitrary"), name="nsa",
    )(qc, qr, gates, kvcmp, kvsel, kvwin)


def _diff_kernel(q_ref, kv_ref, lam_ref, sub_ref, o_ref, *scr, tq, past, lam_init, n_once):
    ch = DIFF_CHUNK
    rep = D_HEADS // D_KVH
    rows = rep * tq
    q0 = past + pl.program_id(1) * tq
    qpos = _row_pos(rows, tq, q0)
    outs = [None] * (2 * D_KVH)

    if n_once:
        mask = lax.broadcasted_iota(I32, (1, n_once), 1) <= qpos
        for g in range(D_KVH):
            kk = kv_ref[0, :n_once, g * LANES:(g + 1) * LANES]
            vv = kv_ref[0, :n_once, (D_KVH + g) * LANES:(D_KVH + g + 1) * LANES]
            for m in range(2):
                idx = g * 2 + m
                outs[idx] = _attend_once(_dot_nt(q_ref[0, 0, idx * rows:(idx + 1) * rows, :], kk), mask, vv)
    else:
        lane = lax.broadcasted_iota(I32, (1, ch), 1)
        states = [scr[3 * k:3 * k + 3] for k in range(2 * D_KVH)]
        for st in states:
            _init_state(st)

        def step(masked, c, carry):
            off = pl.multiple_of(c * ch, ch)
            mask = (off + lane <= qpos) if masked else None
            kv = kv_ref[0, pl.ds(off, ch), :]
            for g in range(D_KVH):
                kk = kv[:, g * LANES:(g + 1) * LANES]
                vv = kv[:, (D_KVH + g) * LANES:(D_KVH + g + 1) * LANES]
                for m in range(2):
                    idx = g * 2 + m
                    _online_update(_dot_nt(q_ref[0, 0, idx * rows:(idx + 1) * rows, :], kk), mask, vv, states[idx])
            return carry

        _causal_loops(step, q0, tq, ch)
        outs = [_finish(st) for st in states]

    lv = lam_ref[...]
    lam = (jnp.exp(jnp.sum(lv[0:1] * lv[1:2], axis=1, keepdims=True))
           - jnp.exp(jnp.sum(lv[2:3] * lv[3:4], axis=1, keepdims=True)) + lam_init)
    for g in range(D_KVH):
        o = outs[2 * g] - lam * outs[2 * g + 1]
        o = _rms(o, sub_ref[...]) * (1.0 - lam_init)
        for r in range(rep):
            head = g * rep + r
            o_ref[0, :, head * LANES:(head + 1) * LANES] = o[r * tq:(r + 1) * tq]


def _diff(q, kv, lam_vec, subln, tq, past, lam_init):
    b, nq, qrows, _ = q.shape
    s_pad = kv.shape[1]
    rows = (D_HEADS // D_KVH) * tq
    n_once = _once_len(nq, rows, past, tq)
    return pl.pallas_call(
        functools.partial(_diff_kernel, tq=tq, past=past, lam_init=lam_init, n_once=n_once),
        grid=(b, nq),
        in_specs=[pl.BlockSpec((1, 1, qrows, LANES), lambda i, j: (i, j, 0, 0)),
                  pl.BlockSpec((1, s_pad, 4 * LANES), lambda i, j: (i, 0, 0)),
                  pl.BlockSpec((4, D_DH), lambda i, j: (0, 0)),
                  pl.BlockSpec((1, 2 * D_DH), lambda i, j: (0, 0))],
        out_specs=pl.BlockSpec((1, tq, D_HEADS * 2 * D_DH), lambda i, j: (i, j, 0)),
        out_shape=jax.ShapeDtypeStruct((b, nq * tq, D_HEADS * 2 * D_DH), F32),
        scratch_shapes=_state_scratch(rows, 0 if n_once else 2 * D_KVH),
        compiler_params=_params("arbitrary", "arbitrary"), name="diff",
    )(q, kv, lam_vec, subln)


def _route(logits):
    lane = lax.broadcasted_iota(I32, logits.shape, 1)
    lane_f = lane.astype(F32)
    big = 1e9
    is_g = lane < N_GROUPS
    gl = jnp.where(is_g, logits, -jnp.inf)
    gmax = jnp.max(gl, axis=1, keepdims=True)
    grp = jnp.min(jnp.where(gl == gmax, lane_f, big), axis=1, keepdims=True)
    g_w = 1.0 / jnp.sum(jnp.where(is_g, jnp.exp(gl - gmax), 0.0), axis=1, keepdims=True)
    e_grp = ((lane - N_GROUPS) >> 2).astype(F32)
    el = jnp.where(e_grp == grp, logits, -jnp.inf)
    el = jnp.where(lane >= N_GROUPS, jnp.where(lane < N_GROUPS + N_EXPERTS, el, -jnp.inf), -jnp.inf)
    v1 = jnp.max(el, axis=1, keepdims=True)
    i1 = jnp.min(jnp.where(el == v1, lane_f, big), axis=1, keepdims=True)
    el2 = jnp.where(lane_f == i1, -jnp.inf, el)
    v2 = jnp.max(el2, axis=1, keepdims=True)
    i2 = jnp.min(jnp.where(el2 == v2, lane_f, big), axis=1, keepdims=True)
    e21 = jnp.exp(v2 - v1)
    w1 = g_w / (1.0 + e21)
    return jnp.where(lane_f == i1, w1, 0.0) + jnp.where(lane_f == i2, w1 * e21, 0.0)


def _moe_kernel(ma_ref, mb_ref, x_ref, m2_ref, m3_ref, m4_ref, m5_ref, g_ref, woa_ref, wob_ref,
                wrh_ref, wrl_ref, wg_ref, wu_ref, wd_ref, gf_ref, o_ref,
                x1_ref, h_ref, comb_ref, acc_ref, *, final_norm):
    bb, tt, d = x_ref.shape
    tm = bb * tt
    e = pl.program_id(2)

    @pl.when(e == 0)
    def _():
        mix = _dot(ma_ref[...], woa_ref[...]) + _dot(mb_ref[...], wob_ref[...])
        x1 = x_ref[...] + m2_ref[...] * mix.reshape(bb, tt, d)
        h = (_rms(x1, g_ref[...]) * (1.0 + m4_ref[...]) + m3_ref[...]).reshape(tm, d)
        x1_ref[...] = x1.reshape(tm, d)
        h_hi = h.astype(BF16)
        h_lo = (h - h_hi.astype(F32)).astype(BF16)
        h_ref[...] = h_hi
        logits = _dot(h_hi, wrh_ref[...]) + (_dot(h_hi, wrl_ref[...]) + _dot(h_lo, wrh_ref[...]))
        comb_ref[...] = _route(logits)
        acc_ref[...] = jnp.zeros(acc_ref.shape, F32)

    hb = h_ref[...]
    a = _dot(hb, wg_ref[0])
    u = _dot(hb, wu_ref[0])
    lane = lax.broadcasted_iota(I32, (tm, LANES), 1)
    ce = jnp.sum(jnp.where(lane == e + N_GROUPS, comb_ref[...], 0.0), axis=1, keepdims=True)
    act = (a * _sigmoid(a)) * u * ce
    acc_ref[...] += _dot(act.astype(BF16), wd_ref[0])

    @pl.when(e == pl.num_programs(2) - 1)
    def _():
        out = x1_ref[...].reshape(bb, tt, d) + m5_ref[...] * acc_ref[...].reshape(bb, tt, d)
        if final_norm:
            out = _rms(out, gf_ref[...])
        o_ref[...] = out


def _moe(mix_a, mix_b, x, mod, g_ffn, woa, wob, wrh, wrl, wg, wu, wd, g_final, final_norm, bb, tt):
    bx, tx, d = x.shape
    tm = bb * tt
    blk = lambda n: pl.BlockSpec((bb, tt, n), lambda i, j, e: (i, j, 0))
    nj = tx // tt
    flat = lambda a: pl.BlockSpec((tm, a.shape[-1]), lambda i, j, e: (i * nj + j, 0))
    modspec = pl.BlockSpec((bb, 1, d), lambda i, j, e: (i, 0, 0))
    vec = pl.BlockSpec((1, 1, d), lambda i, j, e: (0, 0, 0))
    full = lambda a: pl.BlockSpec(a.shape, lambda i, j, e: (0,) * a.ndim)
    n_e, _, f = wg.shape
    return pl.pallas_call(
        functools.partial(_moe_kernel, final_norm=final_norm),
        grid=(bx // bb, tx // tt, n_e),
        in_specs=[flat(mix_a), flat(mix_b), blk(d), modspec, modspec, modspec, modspec, vec,
                  full(woa), full(wob), full(wrh), full(wrl),
                  pl.BlockSpec((1, d, f), lambda i, j, e: (e, 0, 0)),
                  pl.BlockSpec((1, d, f), lambda i, j, e: (e, 0, 0)),
                  pl.BlockSpec((1, f, d), lambda i, j, e: (e, 0, 0)), vec],
        out_specs=blk(d),
        out_shape=jax.ShapeDtypeStruct((bx, tx, d), F32),
        scratch_shapes=[pltpu.VMEM((tm, d), F32), pltpu.VMEM((tm, d), BF16),
                        pltpu.VMEM((tm, LANES), F32), pltpu.VMEM((tm, d), F32)],
        compiler_params=_params("arbitrary", "arbitrary", "arbitrary"), name="moe",
    )(mix_a.reshape(bx * tx, -1), mix_b.reshape(bx * tx, -1), x, mod[2], mod[3], mod[4], mod[5], g_ffn.reshape(1, 1, d),
      woa, wob, wrh, wrl, wg, wu, wd, g_final.reshape(1, 1, d))


def _head_major(x, nq, tq):
    b, _, h, f = x.shape
    return x.reshape(b, nq, tq, h, f).transpose(0, 1, 3, 2, 4).reshape(b, nq, h * tq, f)


def _token_major(x, tq, h):
    b, nq, _, f = x.shape
    return x.reshape(b, nq, h, tq, f).transpose(0, 1, 3, 2, 4).reshape(b, nq * tq, h, f)


def _lane_pad(x, slot, n_slots=2):
    parts = [jnp.where(jnp.asarray(np.asarray(slot) == s)[:, None], x, 0) for s in range(n_slots)]
    return jnp.concatenate(parts, axis=-1)


def _even_mixer(proj, kv_rows, mla_w, tabs, tab_blocks, tq, past, s_total):
    b, t, _ = proj.shape
    nq = t // tq
    ki, akv, kmla = kv_rows
    qi = _head_major(proj[..., EV_QI:EV_QI + 512].astype(BF16).reshape(b, t, IDX_HEADS, IDX_DH), nq, tq)
    wi = proj[..., EV_WI:EV_WI + IDX_HEADS] * (IDX_HEADS ** -0.5 * IDX_DH ** -0.5)
    qa = (proj[..., EV_QA:EV_QA + 512] * A_DH ** -0.5).astype(BF16).reshape(b, t, A_HEADS, A_DH)
    qa = _head_major(_lane_pad(qa, np.arange(A_HEADS) // (A_HEADS // A_KVH)), nq, tq)
    oa = _dsa(qi, wi, qa, ki, akv, tq, past, s_total)
    oa = _token_major(oa, tq, A_HEADS).reshape(b, t, A_HEADS, A_KVH, A_DH)
    hsel = jnp.asarray(np.arange(A_HEADS) // (A_HEADS // A_KVH))
    oa = jnp.take_along_axis(oa, hsel[None, None, :, None, None], axis=3).reshape(b, t, A_HEADS * A_DH)

    q_norm, wn, wr, wuk, wuv = mla_w
    qp = _mlaq(proj[..., EV_QLAT:EV_QLAT + B_QLORA].reshape(b * t, B_QLORA), q_norm, wn, wr, wuk, tabs, tab_blocks)
    if tq == QBLK:
        qp = qp.reshape(b, nq, B_HEADS * tq, 2 * LANES)
    else:
        per = QBLK // tq
        qp = qp.reshape(b // per, B_HEADS, per, tq, 2 * LANES).transpose(0, 2, 1, 3, 4).reshape(b, 1, B_HEADS * tq, 2 * LANES)
    ol = _mla(qp, kmla, tq, past)
    if tq == QBLK:
        ol = ol.reshape(b * nq, B_HEADS, tq, LANES)
    else:
        per = QBLK // tq
        ol = ol.reshape(b // per, per, B_HEADS, tq, LANES).transpose(0, 2, 1, 3, 4).reshape(b // per, B_HEADS, QBLK, LANES)
    ob = _mlao(ol, wuv).reshape(b, t, B_HEADS * B_VDH)
    return oa, ob


def _odd_mixer(proj, kv_rows, cmp_w, diff_w, tq, past, win_base, s_total, lam_init):
    b, t, _ = proj.shape
    nq = t // tq
    cmp_rows, kvsel, kvwin, dkv = kv_rows
    pe_rows, phi_big = cmp_w
    n_cmp = s_total // CMP_BLK
    kvc = _compress(cmp_rows, pe_rows, phi_big, n_cmp)
    nb = -(-n_cmp // (2 * LANES)) * 2 * LANES
    kvc = jnp.pad(kvc.astype(BF16), ((0, 0), (0, nb - n_cmp), (0, 0)))
    kvc = jnp.concatenate([kvc[:, 0::2], kvc[:, 1::2]], axis=1)
    zeros_slot = np.zeros(C_HEADS, np.int64)
    qc = (proj[..., OD_QC:OD_QC + 512] * C_DH ** -0.5).astype(BF16).reshape(b, t, C_HEADS, C_DH)
    qr = (proj[..., OD_QCR:OD_QCR + 512] * C_DH ** -0.5).astype(BF16).reshape(b, t, C_HEADS, C_DH)
    qc = _head_major(_lane_pad(qc, zeros_slot), nq, tq)
    qr = _head_major(_lane_pad(qr, zeros_slot), nq, tq)
    gates = proj[..., OD_GATE:OD_GATE + LANES]
    oc = _nsa(qc, qr, gates, kvc, kvsel, kvwin, tq, past, win_base, s_total)
    oc = _token_major(oc, tq, C_HEADS)[..., C_DH:].reshape(b, t, C_HEADS * C_DH)

    lam_vec, subln = diff_w
    qd = (proj[..., OD_QD:OD_QD + 512] * D_DH ** -0.5).astype(BF16)
    rep = D_HEADS // D_KVH
    qd = qd.reshape(b, t, D_KVH, rep, 2, D_DH).transpose(0, 1, 2, 4, 3, 5)
    qd = qd.reshape(b, t, D_KVH * 2 * rep, D_DH)
    slot = np.tile(np.repeat(np.arange(2), rep), D_KVH)
    qd = _head_major(_lane_pad(qd, slot), nq, tq)
    od = _diff(qd, dkv, lam_vec, subln, tq, past, lam_init)
    return oc, od.astype(BF16)


def kernel(x_prompt, x_sample, c_prompt, c_sample, cache_a_kv, cache_a_idx, cache_b_latent, cache_b_krope,
           cache_c_cmp, cache_c_sel, state_c_win, cache_d_kv, page_table, w_ada, b_ada, g_mix, g_ffn, w_out,
           w_in_even, b_q_norm, b_kv_norm, w_b_qb, w_b_kvb, w_in_odd, c_phi_k, c_phi_v, c_pe_k, c_pe_v,
           d_lambda, d_subln, moe_w_group, moe_w_expert, moe_w_gate, moe_w_up, moe_w_down, g_final):
    depth = w_ada.shape[0]
    bp, tp, d = x_prompt.shape
    bs, ts, _ = x_sample.shape
    n_pages = page_table.shape[1]
    past = n_pages * PAGE_SIZE
    win_buf = state_c_win.shape[2]
    n_pool = cache_a_kv.shape[1]
    assert tp % KEY_CHUNK == 0 and tp % DIFF_CHUNK == 0 and QBLK % ts == 0 and (bs * ts) % QBLK == 0

    bc = -(-(bp + bs) // 8) * 8
    c_all = jnp.pad(jnp.concatenate([c_prompt, c_sample]), ((0, bc - bp - bs), (0, 0)))
    mod_all = _ada(c_all, w_ada, b_ada).reshape(depth, bc, 6, 1, d)

    pos_p = jnp.arange(tp, dtype=I32)
    pos_s = past + jnp.arange(ts, dtype=I32)
    tabs_p = _all_tables(pos_p)
    bb_s = min(bs, 64)
    tabs_s = jnp.tile(_all_tables(pos_s), (1, max(bb_s, QBLK // ts), 1))

    xp, xs = x_prompt, x_sample
    rows = {k: [] for k in ("akv_p", "akv_s", "idx_p", "idx_s", "lat_p", "lat_s", "kr_p", "kr_s",
                            "cmp_p", "cmp_s", "sel_p", "sel_s", "win_p", "win_s", "dkv_p", "dkv_s")}
    tm_p = 512
    for l in range(depth):
        li = l // 2
        mod_p = [mod_all[l, :bp, k] for k in range(6)]
        mod_s = [mod_all[l, bp:bp + bs, k] for k in range(6)]
        if l % 2 == 0:
            w, groups = _even_weights(w_in_even[li])
            kvn = b_kv_norm[li].reshape(1, LANES)
            pp = _proj(xp, mod_p[0], mod_p[1], g_mix[l], w, tabs_p, kvn, groups, EV_OUT, 1, tm_p)
            ps = _proj(xs, mod_s[0], mod_s[1], g_mix[l], w, tabs_s, kvn, groups, EV_OUT, bb_s, ts)
            wqb = w_b_qb[li]
            pad_h = lambda a: jnp.pad(a, ((0, 0), (0, 0), (0, LANES - a.shape[2]))).reshape(a.shape[0], -1)
            wn = pad_h(wqb[..., :B_NOPE]).astype(BF16)
            wr = pad_h(wqb[..., B_NOPE:]).astype(BF16)
            wkvb = w_b_kvb[li]
            wuk = jnp.pad(wkvb[..., :B_NOPE].transpose(1, 2, 0), ((0, 0), (0, LANES - B_NOPE), (0, 0))).astype(BF16)
            wuv = wkvb[..., B_NOPE:].transpose(1, 0, 2)
            wuv = _lane_pad(wuv.transpose(1, 0, 2), np.arange(B_HEADS) % 2).transpose(1, 0, 2).astype(BF16)
            mla_w = (b_q_norm[li].reshape(1, B_QLORA), wn, wr, wuk, wuv)

            def kv_prompt(p):
                kmla = jnp.concatenate([p[..., EV_CKV:EV_CKV + LANES], p[..., EV_KR:EV_KR + B_ROPE],
                                        jnp.zeros(p.shape[:2] + (LANES - B_ROPE,), F32)], axis=-1)
                return (p[..., EV_KI:EV_KI + IDX_DH].astype(BF16), p[..., EV_AKV:EV_AKV + 256].astype(BF16),
                        kmla.astype(BF16))

            oa_p, ob_p = _even_mixer(pp, kv_prompt(pp), mla_w, tabs_p, tp // QBLK, QBLK, 0, tp)
            new_s = (ps[..., EV_KI:EV_KI + IDX_DH], ps[..., EV_AKV:EV_AKV + 256],
                     ps[..., EV_CKV:EV_CKV + LANES], ps[..., EV_KR:EV_KR + B_ROPE])
            ki_s, akv_s, kmla_s = _gather(page_table, li, [
                ([cache_a_idx], [new_s[0]], IDX_DH), ([cache_a_kv], [new_s[1]], 2 * LANES),
                ([cache_b_latent, cache_b_krope], [new_s[2], new_s[3]], 2 * LANES)])
            oa_s, ob_s = _even_mixer(ps, (ki_s, akv_s, kmla_s), mla_w, tabs_s, 1, ts, past, past + ts)
            for tag, p in (("p", pp), ("s", ps)):
                rows["akv_" + tag].append(p[..., EV_AKV:EV_AKV + 256].reshape(p.shape[:2] + (2, A_KVH, A_DH)))
                rows["idx_" + tag].append(p[..., EV_KI:EV_KI + IDX_DH])
                rows["lat_" + tag].append(p[..., EV_CKV:EV_CKV + B_KVLORA])
                rows["kr_" + tag].append(p[..., EV_KR:EV_KR + B_ROPE])
            mix_p, mix_s = (oa_p, ob_p), (oa_s, ob_s)
        else:
            w, groups = _odd_weights(w_in_odd[li])
            kvn = jnp.ones((1, LANES), F32)
            pp = _proj(xp, mod_p[0], mod_p[1], g_mix[l], w, tabs_p, kvn, groups, OD_OUT, 1, tm_p)
            ps = _proj(xs, mod_s[0], mod_s[1], g_mix[l], w, tabs_s, kvn, groups, OD_OUT, bb_s, ts)
            lam_init = 0.8 - 0.6 * math.exp(-0.3 * l)
            zk = jnp.zeros_like(c_phi_k[li])
            phi_big = jnp.concatenate([jnp.stack([c_phi_k[li], zk], axis=1), jnp.stack([zk, c_phi_v[li]], axis=1)],
                                      axis=-1).reshape(CMP_BLK * 2 * C_DH, 2 * C_DH).astype(BF16)
            pe = jnp.stack([c_pe_k[li], c_pe_v[li]], axis=1).reshape(1, CMP_BLK * 2 * C_DH)
            cmp_w = (jnp.tile(pe, (8, 1)).astype(BF16), phi_big)
            diff_w = (d_lambda[li], d_subln[li].reshape(1, 2 * D_DH))
            blk_rows = lambda a: a.reshape(a.shape[0], a.shape[1] // CMP_BLK, CMP_BLK * 2 * C_DH)
            kv_p = (blk_rows(pp[..., OD_CMP:OD_CMP + LANES].astype(BF16)),
                    pp[..., OD_SEL:OD_SEL + LANES].astype(BF16), pp[..., OD_WIN:OD_WIN + LANES].astype(BF16),
                    pp[..., OD_DKV:OD_DKV + 512].astype(BF16))
            oc_p, od_p = _odd_mixer(pp, kv_p, cmp_w, diff_w, QBLK, 0, 0, tp, lam_init)
            flat = lambda a: a.reshape(a.shape[:3] + (-1,))
            new_cmp, new_sel = ps[..., OD_CMP:OD_CMP + LANES], ps[..., OD_SEL:OD_SEL + LANES]
            new_win, new_dkv = ps[..., OD_WIN:OD_WIN + LANES], ps[..., OD_DKV:OD_DKV + 512]
            cmp_s, sel_s, dkv_s = _gather(page_table, li, [
                ([cache_c_cmp], [new_cmp], LANES), ([cache_c_sel], [new_sel], LANES),
                ([cache_d_kv], [new_dkv], 4 * LANES)])
            win_full = jnp.concatenate([state_c_win[li].reshape(bs, win_buf, LANES), new_win], axis=1)
            w_rows = -(-(win_buf + ts) // LANES) * LANES
            win_s = jnp.pad(win_full, ((0, 0), (0, w_rows - win_buf - ts), (0, 0))).astype(BF16)
            oc_s, od_s = _odd_mixer(ps, (blk_rows(cmp_s), sel_s, win_s, dkv_s), cmp_w, diff_w,
                                    ts, past, past - win_buf, past + ts, lam_init)
            keep_p = min(WINDOW, tp)
            rows["win_p"].append(pp[:, tp - keep_p:, OD_WIN:OD_WIN + LANES].reshape(bp, keep_p, 2, C_DH))
            rows["win_s"].append(win_full[:, ts:].reshape(bs, win_buf, 2, C_DH))
            for tag, p in (("p", pp), ("s", ps)):
                rows["cmp_" + tag].append(p[..., OD_CMP:OD_CMP + LANES].reshape(p.shape[:2] + (2, C_DH)))
                rows["sel_" + tag].append(p[..., OD_SEL:OD_SEL + LANES].reshape(p.shape[:2] + (2, C_DH)))
                rows["dkv_" + tag].append(p[..., OD_DKV:OD_DKV + 512].reshape(p.shape[:2] + (2, D_KVH, 2 * D_DH)))
            mix_p, mix_s = (oc_p, od_p), (oc_s, od_s)

        woa, wob = w_out[l, :512].astype(BF16), w_out[l, 512:].astype(BF16)
        wr_full = jnp.concatenate([moe_w_group[l], moe_w_expert[l],
                                   jnp.zeros((d, LANES - N_GROUPS - N_EXPERTS), F32)], axis=1)
        wrh = wr_full.astype(BF16)
        wrl = (wr_full - wrh.astype(F32)).astype(BF16)
        wg, wu, wd = moe_w_gate[l].astype(BF16), moe_w_up[l].astype(BF16), moe_w_down[l].astype(BF16)
        last = l == depth - 1
        xp = _moe(mix_p[0], mix_p[1], xp, mod_p, g_ffn[l], woa, wob, wrh, wrl, wg, wu, wd, g_final, last,
                  1, min(tp, 1024))
        xs = _moe(mix_s[0], mix_s[1], xs, mod_s, g_ffn[l], woa, wob, wrh, wrl, wg, wu, wd, g_final, last,
                  bs, ts)

    st = lambda k: jnp.stack(rows[k])
    return (xp, xs, st("akv_p"), st("akv_s"), st("idx_p"), st("idx_s"), st("lat_p"), st("lat_s"),
            st("kr_p"), st("kr_s"), st("cmp_p"), st("cmp_s"), st("sel_p"), st("sel_s"),
            st("win_p"), st("win_s"), st("dkv_p"), st("dkv_s"))
```
